```python
import math
import jax, jax.numpy as jnp
from jax import lax
import numpy as np

D_MODEL = 4096
BATCH = 2
SEQ = 4096
DEPTH = 2

HEAD_DIM = 128
A_GROUPS = ((128, 1), (512, 4), (2048, 16))
A_HEADS = D_MODEL // 256
B_HEADS = D_MODEL // HEAD_DIM
D_FF = ((8 * D_MODEL // 3 + 255) // 256) * 256
N_EXPERTS = 8
TOP_K = 2
D_EXPERT = 3 * D_MODEL // 2
PLE_DIM = 256
ROPE_THETA = 10000.0
EPS = 1e-6
BLOCK = 128
N_A = DEPTH // 2
N_B = DEPTH - N_A
N_DENSE = (DEPTH + 1) // 2
N_MOE = DEPTH // 2

kernel_name = 'yoco_dilated_stickbreaking_moe_block'


def rms_norm(x, g):
    xf = x.astype(jnp.float32)
    y = xf * lax.rsqrt(jnp.mean(xf * xf, axis=-1, keepdims=True) + EPS)
    return (y * g.astype(jnp.float32)).astype(x.dtype)


def rope(x, pos):
    half = HEAD_DIM // 2
    inv = ROPE_THETA ** (-jnp.arange(half, dtype=jnp.float32) / half)
    ang = pos.astype(jnp.float32)[:, None] * inv[None, :]
    cos = jnp.cos(ang)[None, :, None, None, :]
    sin = jnp.sin(ang)[None, :, None, None, :]
    xf = x.astype(jnp.float32)
    x1, x2 = xf[..., :half], xf[..., half:]
    return jnp.concatenate([x1 * cos - x2 * sin, x2 * cos + x1 * sin], axis=-1)


def dilated_window_attention(q, k, v, span, dil):
    Bsz, S, H, D = q.shape
    L = S // dil
    nb = -(-L // BLOCK)
    Lp = nb * BLOCK
    w = span // dil

    def to_sub(t):
        return t.reshape(Bsz, L, dil, H, D).transpose(0, 2, 1, 3, 4)

    qs = jnp.pad(to_sub(q), ((0, 0), (0, 0), (0, Lp - L), (0, 0), (0, 0)))
    qs = qs.reshape(Bsz, dil, nb, BLOCK, H, D)

    def key_blocks(t):
        ts = jnp.pad(to_sub(t), ((0, 0), (0, 0), (BLOCK, Lp - L), (0, 0), (0, 0)))
        ts = ts.reshape(Bsz, dil, nb + 1, BLOCK, H, D)
        return jnp.concatenate([ts[:, :, :-1], ts[:, :, 1:]], axis=3)

    ks, vs = key_blocks(k), key_blocks(v)
    s = jnp.einsum('bdnqhc,bdnkhc->bdnhqk', qs, ks) / math.sqrt(D)
    i = jnp.arange(BLOCK)[:, None]
    j = jnp.arange(2 * BLOCK)[None, :]
    dist = BLOCK + i - j
    key_idx = jnp.arange(nb)[:, None, None] * BLOCK + j[None] - BLOCK
    mask = (dist >= 0) & (dist <= w) & (key_idx >= 0)
    s = jnp.where(mask[:, None], s, -jnp.inf)
    m = jnp.max(s, axis=-1, keepdims=True)
    pexp = jnp.exp(s - m)
    l = jnp.sum(pexp, axis=-1)
    o = jnp.einsum('bdnhqk,bdnkhc->bdnqhc', pexp, vs)
    o = o / jnp.transpose(l, (0, 1, 2, 4, 3))[..., None]
    lse = jnp.transpose(m[..., 0] + jnp.log(l), (0, 1, 2, 4, 3))
    o = o.reshape(Bsz, dil, Lp, H, D)[:, :, :L].transpose(0, 2, 1, 3, 4).reshape(Bsz, S, H, D)
    lse = lse.reshape(Bsz, dil, Lp, H)[:, :, :L].transpose(0, 2, 1, 3).reshape(Bsz, S, H)
    return o, lse


def dilated_mixer(hn, w_qkv, g_q, g_k, w_o, pos):
    Bsz, S, _ = hn.shape
    G = len(A_GROUPS)
    qkv = (hn @ w_qkv).reshape(Bsz, S, 3, G, A_HEADS, HEAD_DIM)
    q = rope(rms_norm(qkv[:, :, 0], g_q), pos)
    k = rope(rms_norm(qkv[:, :, 1], g_k), pos)
    v = qkv[:, :, 2].astype(jnp.float32)
    outs, lses = [], []
    for g, (span, dil) in enumerate(A_GROUPS):
        o_g, l_g = dilated_window_attention(q[:, :, g], k[:, :, g], v[:, :, g], span, dil)
        outs.append(o_g)
        lses.append(l_g)
    o = jnp.stack(outs, axis=2)
    wgt = jax.nn.softmax(jnp.stack(lses, axis=2), axis=2)
    o = jnp.einsum('bsgh,bsghd->bshd', wgt, o).reshape(Bsz, S, A_HEADS * HEAD_DIM)
    return o.astype(hn.dtype) @ w_o


def stick_breaking(q, k, v):
    Bsz, S, H, D = q.shape
    nq = S // BLOCK
    qf = q.astype(jnp.float32).reshape(Bsz, nq, BLOCK, H, D).transpose(1, 0, 3, 2, 4)
    kf = k.astype(jnp.float32).transpose(0, 2, 1, 3)
    vf = v.astype(jnp.float32).transpose(0, 2, 1, 3)
    key_pos = jnp.arange(S)
    scale = 1.0 / math.sqrt(D)

    def block(args):
        qb, n = args
        z = jnp.einsum('bhqd,bhkd->bhqk', qb, kf) * scale
        qpos = n * BLOCK + jnp.arange(BLOCK)
        strict = key_pos[None, :] < qpos[:, None]
        log_keep = jnp.where(strict, jax.nn.log_sigmoid(-z), 0.0)
        later = lax.cumsum(log_keep, axis=3, reverse=True) - log_keep
        a = jnp.where(strict, jnp.exp(jax.nn.log_sigmoid(z) + later), 0.0)
        return jnp.einsum('bhqk,bhkd->bhqd', a, vf)

    o = lax.map(block, (qf, jnp.arange(nq)))
    return o.transpose(1, 0, 3, 2, 4).reshape(Bsz, S, H * D).astype(q.dtype)


def swiglu(x, w_gate, w_up, w_down):
    return (jax.nn.silu(x @ w_gate) * (x @ w_up)) @ w_down


def moe_swiglu(hn, w_router, w_gate, w_up, w_down):
    logits = hn.astype(jnp.float32) @ w_router.astype(jnp.float32)
    top_v, top_i = lax.top_k(logits, TOP_K)
    probs = jax.nn.softmax(top_v, axis=-1)
    gates = jnp.sum(jax.nn.one_hot(top_i, N_EXPERTS, dtype=jnp.float32) * probs[..., None], axis=-2)
    y = jnp.zeros_like(hn)
    for e in range(N_EXPERTS):
        y = y + gates[..., e:e + 1].astype(hn.dtype) * swiglu(hn, w_gate[e], w_up[e], w_down[e])
    return y


def per_layer_embedding(h, p_i, g_norm, w_up, w_gdown, w_gup):
    gate = jax.nn.sigmoid((rms_norm(h, g_norm) @ w_gdown) @ w_gup)
    return h + gate * (p_i @ w_up)


def setup_inputs(seed: int = 0) -> dict:
    key = jax.random.key(seed)
    ks = jax.random.split(key, 24)

    def w(k, shape, fan_in):
        return jax.random.normal(k, shape, jnp.float32) * (fan_in ** -0.5)

    def gain(k, shape):
        return 1.0 + 0.02 * jax.random.normal(k, shape, jnp.float32)

    a_qkv = 3 * len(A_GROUPS) * A_HEADS * HEAD_DIM
    a_out = A_HEADS * HEAD_DIM
    b_w = B_HEADS * HEAD_DIM
    return {
        'x': jax.random.normal(ks[0], (BATCH, SEQ, D_MODEL), jnp.float32),
        'p': jax.random.normal(ks[1], (DEPTH, BATCH, SEQ, PLE_DIM), jnp.float32),
        'norm_mix': gain(ks[2], (DEPTH, D_MODEL)),
        'norm_ffn': gain(ks[3], (DEPTH, D_MODEL)),
        'norm_ple': gain(ks[4], (DEPTH, D_MODEL)),
        'a_w_qkv': w(ks[5], (N_A, D_MODEL, a_qkv), D_MODEL),
        'a_q_norm': gain(ks[6], (N_A, HEAD_DIM)),
        'a_k_norm': gain(ks[7], (N_A, HEAD_DIM)),
        'a_w_o': w(ks[8], (N_A, a_out, D_MODEL), a_out),
        'kv_norm': gain(ks[9], (D_MODEL,)),
        'w_kv': w(ks[10], (D_MODEL, 2 * b_w), D_MODEL),
        'b_w_q': w(ks[11], (N_B, D_MODEL, b_w), D_MODEL),
        'b_w_o': w(ks[12], (N_B, b_w, D_MODEL), b_w),
        'ffn_w_gate': w(ks[13], (N_DENSE, D_MODEL, D_FF), D_MODEL),
        'ffn_w_up': w(ks[14], (N_DENSE, D_MODEL, D_FF), D_MODEL),
        'ffn_w_down': w(ks[15], (N_DENSE, D_FF, D_MODEL), D_FF),
        'moe_w_router': w(ks[16], (N_MOE, D_MODEL, N_EXPERTS), D_MODEL),
        'moe_w_gate': w(ks[17], (N_MOE, N_EXPERTS, D_MODEL, D_EXPERT), D_MODEL),
        'moe_w_up': w(ks[18], (N_MOE, N_EXPERTS, D_MODEL, D_EXPERT), D_MODEL),
        'moe_w_down': w(ks[19], (N_MOE, N_EXPERTS, D_EXPERT, D_MODEL), D_EXPERT),
        'ple_w_up': w(ks[20], (DEPTH, PLE_DIM, D_MODEL), PLE_DIM),
        'ple_w_gdown': w(ks[21], (DEPTH, D_MODEL, PLE_DIM), D_MODEL),
        'ple_w_gup': w(ks[22], (DEPTH, PLE_DIM, D_MODEL), PLE_DIM),
    }


def reference(x, p, norm_mix, norm_ffn, norm_ple, a_w_qkv, a_q_norm, a_k_norm, a_w_o,
              kv_norm, w_kv, b_w_q, b_w_o, ffn_w_gate, ffn_w_up, ffn_w_down,
              moe_w_router, moe_w_gate, moe_w_up, moe_w_down,
              ple_w_up, ple_w_gdown, ple_w_gup):
    Bsz, S, _ = x.shape
    pos = jnp.arange(S)
    h = x
    k_s = v_s = None
    for i in range(DEPTH):
        if i == N_A:
            kv = (rms_norm(h, kv_norm) @ w_kv).reshape(Bsz, S, 2, B_HEADS, HEAD_DIM)
            k_s, v_s = kv[:, :, 0], kv[:, :, 1]
        hn = rms_norm(h, norm_mix[i])
        if i < N_A:
            h = h + dilated_mixer(hn, a_w_qkv[i], a_q_norm[i], a_k_norm[i], a_w_o[i], pos)
        else:
            j = i - N_A
            q = (hn @ b_w_q[j]).reshape(Bsz, S, B_HEADS, HEAD_DIM)
            h = h + stick_breaking(q, k_s, v_s) @ b_w_o[j]
        hn = rms_norm(h, norm_ffn[i])
        if i % 2 == 0:
            c = i // 2
            h = h + swiglu(hn, ffn_w_gate[c], ffn_w_up[c], ffn_w_down[c])
        else:
            c = i // 2
            h = h + moe_swiglu(hn, moe_w_router[c], moe_w_gate[c], moe_w_up[c], moe_w_down[c])
        h = per_layer_embedding(h, p[i], norm_ple[i], ple_w_up[i], ple_w_gdown[i], ple_w_gup[i])
    return h
```

```python
import functools
import math

import jax
import jax.numpy as jnp
from jax import lax
from jax.experimental import pallas as pl
from jax.experimental.pallas import tpu as pltpu

F32 = jnp.float32
BF16 = jnp.bfloat16

EPS = 1e-6
HEAD_DIM = 128
ROPE_THETA = 10000.0
A_GROUPS = ((128, 1), (512, 4), (2048, 16))
N_EXPERTS = 8
TOP_K = 2

LANES = 128
ATTN_BLOCK = 128
SB_BLOCK = 256
DOWN_CHUNK = 512
VMEM_LIMIT = 56 * 1024 * 1024

_NT = (((1,), (1,)), ((), ()))


def _params(*semantics):
    return pltpu.CompilerParams(dimension_semantics=semantics, vmem_limit_bytes=VMEM_LIMIT)


def _sigmoid(x):
    return 1.0 / (1.0 + jnp.exp(-x))


def _norm_kernel(*refs, has_delta, n_out):
    refs = list(refs)
    h_ref = refs.pop(0)
    d_ref = refs.pop(0) if has_delta else None
    g_refs = [refs.pop(0) for _ in range(n_out)]
    hsum_ref = refs.pop(0) if has_delta else None
    o_refs = refs
    h = h_ref[...]
    if has_delta:
        h = h + d_ref[...]
        hsum_ref[...] = h
    y = h * lax.rsqrt(jnp.mean(h * h, axis=-1, keepdims=True) + EPS)
    for g_ref, o_ref in zip(g_refs, o_refs):
        o_ref[...] = (y * g_ref[...]).astype(o_ref.dtype)


def _rmsnorm(h, delta, gains, out_dtypes, tm=256):
    T, D = h.shape
    has_delta = delta is not None
    row = pl.BlockSpec((tm, D), lambda i: (i, 0))
    gain = pl.BlockSpec((1, D), lambda i: (0, 0))
    ins = [h] + ([delta] if has_delta else []) + [g.reshape(1, D) for g in gains]
    in_specs = [row] * (2 if has_delta else 1) + [gain] * len(gains)
    out_shape = ([jax.ShapeDtypeStruct((T, D), F32)] if has_delta else []) + [
        jax.ShapeDtypeStruct((T, D), dt) for dt in out_dtypes]
    outs = pl.pallas_call(
        functools.partial(_norm_kernel, has_delta=has_delta, n_out=len(gains)),
        grid=(T // tm,),
        in_specs=in_specs,
        out_specs=[row] * len(out_shape),
        out_shape=out_shape,
        compiler_params=_params("parallel"),
        name="rmsnorm",
    )(*ins)
    return list(outs)


def _mm_kernel(x_ref, w_ref, *rest, has_res):
    acc = jnp.dot(x_ref[...], w_ref[...].astype(BF16), preferred_element_type=F32)
    if has_res:
        r_ref, o_ref = rest
        acc = r_ref[...] + acc
    else:
        (o_ref,) = rest
    o_ref[...] = acc.astype(o_ref.dtype)


def _matmul(x, w, out_dtype, residual=None, tm=1024, tn=512):
    M, K = x.shape
    N = w.shape[1]
    has_res = residual is not None
    in_specs = [pl.BlockSpec((tm, K), lambda i, j: (i, 0)),
                pl.BlockSpec((K, tn), lambda i, j: (0, j))]
    ins = [x, w]
    if has_res:
        in_specs.append(pl.BlockSpec((tm, tn), lambda i, j: (i, j)))
        ins.append(residual)
    return pl.pallas_call(
        functools.partial(_mm_kernel, has_res=has_res),
        grid=(M // tm, N // tn),
        in_specs=in_specs,
        out_specs=pl.BlockSpec((tm, tn), lambda i, j: (i, j)),
        out_shape=jax.ShapeDtypeStruct((M, N), out_dtype),
        compiler_params=_params("parallel", "arbitrary"),
        name="matmul",
    )(*ins)


def _qkv_kernel(x_ref, w_ref, cos_ref, sin_ref, gq_ref, gk_ref, o_ref, *, q_tiles, qk_tiles):
    j = pl.program_id(1)
    acc = jnp.dot(x_ref[...], w_ref[...].astype(BF16), preferred_element_type=F32)

    @pl.when(j < qk_tiles)
    def _():
        gain = jnp.where(j < q_tiles, gq_ref[...], gk_ref[...])
        cos = cos_ref[...]
        sin = sin_ref[...]
        for hh in range(acc.shape[1] // HEAD_DIM):
            sl = slice(hh * HEAD_DIM, (hh + 1) * HEAD_DIM)
            blk = acc[:, sl]
            y = blk * lax.rsqrt(jnp.mean(blk * blk, axis=-1, keepdims=True) + EPS) * gain
            o_ref[:, sl] = (y * cos + pltpu.roll(y, HEAD_DIM // 2, 1) * sin).astype(o_ref.dtype)

    @pl.when(j >= qk_tiles)
    def _():
        o_ref[...] = acc.astype(o_ref.dtype)


def _rope_tables(S):
    half = HEAD_DIM // 2
    inv = ROPE_THETA ** (-jnp.arange(half, dtype=F32) / half)
    ang = jnp.arange(S, dtype=F32)[:, None] * inv[None, :]
    cos, sin = jnp.cos(ang), jnp.sin(ang)
    return jnp.concatenate([cos, cos], axis=-1), jnp.concatenate([-sin, sin], axis=-1)


def _qkv_proj(x, w, g_q, g_k, S, tm=1024, tn=512):
    M, K = x.shape
    N = w.shape[1]
    cos, sin = _rope_tables(S)
    pos_blocks = S // tm
    third = N // 3
    return pl.pallas_call(
        functools.partial(_qkv_kernel, q_tiles=third // tn, qk_tiles=2 * third // tn),
        grid=(M // tm, N // tn),
        in_specs=[pl.BlockSpec((tm, K), lambda i, j: (i, 0)),
                  pl.BlockSpec((K, tn), lambda i, j: (0, j)),
                  pl.BlockSpec((tm, HEAD_DIM), lambda i, j: (i % pos_blocks, 0)),
                  pl.BlockSpec((tm, HEAD_DIM), lambda i, j: (i % pos_blocks, 0)),
                  pl.BlockSpec((1, HEAD_DIM), lambda i, j: (0, 0)),
                  pl.BlockSpec((1, HEAD_DIM), lambda i, j: (0, 0))],
        out_specs=pl.BlockSpec((tm, tn), lambda i, j: (i, j)),
        out_shape=jax.ShapeDtypeStruct((M, N), BF16),
        compiler_params=_params("parallel", "arbitrary"),
        name="qkv_proj",
    )(x, w, cos, sin, g_q.reshape(1, HEAD_DIM), g_k.reshape(1, HEAD_DIM))


def _dil_attn_kernel(q_ref, kp_ref, kc_ref, vp_ref, vc_ref, o_ref, st_ref, *, n_heads, scale):
    n = pl.program_id(2)
    blk = q_ref.shape[1]
    i = lax.broadcasted_iota(jnp.int32, (blk, blk), 0)
    j = lax.broadcasted_iota(jnp.int32, (blk, blk), 1)
    mask_p = (j >= i) & (n > 0)
    mask_c = j <= i
    lane = lax.broadcasted_iota(jnp.int32, (blk, LANES), 1)
    stats = jnp.zeros((blk, LANES), F32)
    for h in range(n_heads):
        sl = slice(h * HEAD_DIM, (h + 1) * HEAD_DIM)
        q = q_ref[0, :, sl]
        s_p = lax.dot_general(q, kp_ref[0, :, sl], _NT, preferred_element_type=F32) * scale
        s_c = lax.dot_general(q, kc_ref[0, :, sl], _NT, preferred_element_type=F32) * scale
        s_p = jnp.where(mask_p, s_p, -jnp.inf)
        s_c = jnp.where(mask_c, s_c, -jnp.inf)
        m = jnp.maximum(jnp.max(s_p, axis=-1, keepdims=True), jnp.max(s_c, axis=-1, keepdims=True))
        p_p = jnp.exp(s_p - m)
        p_c = jnp.exp(s_c - m)
        l = jnp.sum(p_p, axis=-1, keepdims=True) + jnp.sum(p_c, axis=-1, keepdims=True)
        o = (jnp.dot(p_p.astype(BF16), vp_ref[0, :, sl], preferred_element_type=F32)
             + jnp.dot(p_c.astype(BF16), vc_ref[0, :, sl], preferred_element_type=F32))
        o_ref[0, :, sl] = (o / l).astype(o_ref.dtype)
        stats = jnp.where(lane == h, m + jnp.log(l), stats)
    st_ref[0] = stats


def _dilated_attention_group(qkv_g, B, S, dil, n_heads):
    hw = n_heads * HEAD_DIM
    L = S // dil
    nb = L // ATTN_BLOCK
    a = qkv_g.reshape(B, L, dil * 3 * hw)
    blk = (1, ATTN_BLOCK, hw)
    cur = lambda c: (lambda b, r, n: (b, n, r * 3 + c))
    prev = lambda c: (lambda b, r, n: (b, jnp.maximum(n - 1, 0), r * 3 + c))
    o, st = pl.pallas_call(
        functools.partial(_dil_attn_kernel, n_heads=n_heads, scale=1.0 / math.sqrt(HEAD_DIM)),
        grid=(B, dil, nb),
        in_specs=[pl.BlockSpec(blk, cur(0)),
                  pl.BlockSpec(blk, prev(1)), pl.BlockSpec(blk, cur(1)),
                  pl.BlockSpec(blk, prev(2)), pl.BlockSpec(blk, cur(2))],
        out_specs=[pl.BlockSpec(blk, lambda b, r, n: (b, n, r)),
                   pl.BlockSpec((1, ATTN_BLOCK, LANES), lambda b, r, n: (b, n, r))],
        out_shape=[jax.ShapeDtypeStruct((B, L, dil * hw), BF16),
                   jax.ShapeDtypeStruct((B, L, dil * LANES), F32)],
        compiler_params=_params("parallel", "parallel", "arbitrary"),
        name=f"dilated_attn_d{dil}",
    )(a, a, a, a, a)
    return o.reshape(B * S, hw), st.reshape(B * S, LANES)


def _merge_kernel(o0_ref, o1_ref, o2_ref, s0_ref, s1_ref, s2_ref, out_ref, *, n_heads):
    s = [s0_ref[...], s1_ref[...], s2_ref[...]]
    m = jnp.maximum(jnp.maximum(s[0], s[1]), s[2])
    e = [jnp.exp(x - m) for x in s]
    den = e[0] + e[1] + e[2]
    w = [x / den for x in e]
    o_refs = (o0_ref, o1_ref, o2_ref)
    for h in range(n_heads):
        sl = slice(h * HEAD_DIM, (h + 1) * HEAD_DIM)
        acc = w[0][:, h:h + 1] * o_refs[0][:, sl].astype(F32)
        for g in (1, 2):
            acc = acc + w[g][:, h:h + 1] * o_refs[g][:, sl].astype(F32)
        out_ref[:, sl] = acc.astype(out_ref.dtype)


def _merge_groups(os, sts, n_heads, tm=512):
    T, hw = os[0].shape
    row = pl.BlockSpec((tm, hw), lambda i: (i, 0))
    strow = pl.BlockSpec((tm, LANES), lambda i: (i, 0))
    return pl.pallas_call(
        functools.partial(_merge_kernel, n_heads=n_heads),
        grid=(T // tm,),
        in_specs=[row] * 3 + [strow] * 3,
        out_specs=row,
        out_shape=jax.ShapeDtypeStruct((T, hw), BF16),
        compiler_params=_params("parallel"),
        name="merge_groups",
    )(*os, *sts)


def _sb_kernel(q_ref, k_ref, v_ref, o_ref, *, scale):
    qi = pl.program_id(2)
    tq = q_ref.shape[1]
    q = q_ref[0]
    row = lax.broadcasted_iota(jnp.int32, (tq, tq), 0)
    col = lax.broadcasted_iota(jnp.int32, (tq, tq), 1)
    strict = col < row
    tri = (row > col).astype(BF16)

    def tile(kb, carry, diag):
        acc, run = carry
        start = pl.multiple_of(kb * tq, tq)
        k = k_ref[0, pl.ds(start, tq), :]
        v = v_ref[0, pl.ds(start, tq), :]
        z = lax.dot_general(q, k, _NT, preferred_element_type=F32) * scale
        e = jnp.log(1.0 + jnp.exp(-jnp.abs(z)))
        lk = -(jnp.maximum(z, 0.0) + e)
        ls = jnp.minimum(z, 0.0) - e
        if diag:
            lk = jnp.where(strict, lk, 0.0)
        hi = lk.astype(BF16)
        lo = (lk - hi.astype(F32)).astype(BF16)
        later = (jnp.dot(hi, tri, preferred_element_type=F32)
                 + jnp.dot(lo, tri, preferred_element_type=F32)) + run
        a = jnp.exp(ls + later)
        if diag:
            a = jnp.where(strict, a, 0.0)
        acc = acc + jnp.dot(a.astype(BF16), v, preferred_element_type=F32)
        run = run + jnp.sum(lk, axis=1, keepdims=True)
        return acc, run

    carry = (jnp.zeros((tq, HEAD_DIM), F32), jnp.zeros((tq, 1), F32))
    carry = tile(qi, carry, True)
    carry = lax.fori_loop(0, qi, lambda t, c: tile(qi - 1 - t, c, False), carry)
    o_ref[0] = carry[0].astype(o_ref.dtype)


def _stick_breaking(q, kv, B, S, n_heads):
    hw = n_heads * HEAD_DIM
    q3 = q.reshape(B, S, hw)
    kv3 = kv.reshape(B, S, 2 * hw)
    tq = SB_BLOCK
    out = pl.pallas_call(
        functools.partial(_sb_kernel, scale=1.0 / math.sqrt(HEAD_DIM)),
        grid=(B, n_heads, S // tq),
        in_specs=[pl.BlockSpec((1, tq, HEAD_DIM), lambda b, h, i: (b, i, h)),
                  pl.BlockSpec((1, S, HEAD_DIM), lambda b, h, i: (b, 0, h)),
                  pl.BlockSpec((1, S, HEAD_DIM), lambda b, h, i: (b, 0, n_heads + h))],
        out_specs=pl.BlockSpec((1, tq, HEAD_DIM), lambda b, h, i: (b, i, h)),
        out_shape=jax.ShapeDtypeStruct((B, S, hw), BF16),
        compiler_params=_params("parallel", "parallel", "arbitrary"),
        name="stick_breaking",
    )(q3, kv3, kv3)
    return out.reshape(B * S, hw)


def _swiglu_kernel(te_ref, tv_ref, x_ref, wg_ref, wu_ref, wd_ref, sc_ref, o_ref, *, use_scale):
    t = pl.program_id(0)
    f = pl.program_id(1)
    valid = tv_ref[t] != 0

    @pl.when(valid)
    def _():
        x = x_ref[...]
        g = jnp.dot(x, wg_ref[...].astype(BF16), preferred_element_type=F32)
        u = jnp.dot(x, wu_ref[...].astype(BF16), preferred_element_type=F32)
        mid = (g * _sigmoid(g) * u).astype(BF16)

        @pl.when(f == 0)
        def _():
            o_ref[...] = jnp.zeros_like(o_ref)

        for c in range(0, o_ref.shape[1], DOWN_CHUNK):
            sl = slice(c, c + DOWN_CHUNK)
            o_ref[:, sl] += jnp.dot(mid, wd_ref[:, sl].astype(BF16), preferred_element_type=F32)

        if use_scale:
            @pl.when(f == pl.num_programs(1) - 1)
            def _():
                o_ref[...] *= sc_ref[...]

    @pl.when(jnp.logical_not(valid) & (f == 0))
    def _():
        o_ref[...] = jnp.zeros_like(o_ref)


def _swiglu(x, w_gate, w_up, w_down, tile_expert, tile_valid, row_scale, tm=1024, tf=256):
    P, D = x.shape
    E, _, F = w_gate.shape
    nf = F // tf
    use_scale = row_scale is not None
    if not use_scale:
        row_scale = jnp.ones((P, 1), F32)
    fidx = lambda t, f, tv: jnp.where(tv[t] != 0, f, nf - 1)
    grid_spec = pltpu.PrefetchScalarGridSpec(
        num_scalar_prefetch=2,
        grid=(P // tm, nf),
        in_specs=[pl.BlockSpec((tm, D), lambda t, f, te, tv: (t, 0), pipeline_mode=pl.Buffered(1)),
                  pl.BlockSpec((None, D, tf), lambda t, f, te, tv: (te[t], 0, fidx(t, f, tv))),
                  pl.BlockSpec((None, D, tf), lambda t, f, te, tv: (te[t], 0, fidx(t, f, tv))),
                  pl.BlockSpec((None, tf, D), lambda t, f, te, tv: (te[t], fidx(t, f, tv), 0)),
                  pl.BlockSpec((tm, 1), lambda t, f, te, tv: (t, 0))],
        out_specs=pl.BlockSpec((tm, D), lambda t, f, te, tv: (t, 0), pipeline_mode=pl.Buffered(1)),
    )
    return pl.pallas_call(
        functools.partial(_swiglu_kernel, use_scale=use_scale),
        grid_spec=grid_spec,
        out_shape=jax.ShapeDtypeStruct((P, D), F32),
        compiler_params=_params("parallel", "arbitrary"),
        name="swiglu",
    )(tile_expert, tile_valid, x, w_gate, w_up, w_down, row_scale)


def _router_kernel(x_ref, w_ref, idx_ref, p_ref, *, n_experts):
    logits = jnp.dot(x_ref[...], w_ref[...], preferred_element_type=F32,
                     precision=lax.Precision.HIGHEST)
    lane = lax.broadcasted_iota(jnp.int32, logits.shape, 1)
    logits = jnp.where(lane < n_experts, logits, -jnp.inf)
    m1 = jnp.max(logits, axis=-1, keepdims=True)
    i1 = jnp.min(jnp.where(logits == m1, lane, LANES), axis=-1, keepdims=True)
    rest = jnp.where(lane == i1, -jnp.inf, logits)
    m2 = jnp.max(rest, axis=-1, keepdims=True)
    i2 = jnp.min(jnp.where(rest == m2, lane, LANES), axis=-1, keepdims=True)
    e = jnp.exp(m2 - m1)
    p1 = 1.0 / (1.0 + e)
    p2 = e / (1.0 + e)
    idx_ref[...] = jnp.where(lane == 0, i1, jnp.where(lane == 1, i2, 0))
    p_ref[...] = jnp.where(lane == 0, p1, jnp.where(lane == 1, p2, 0.0))


def _router(x, w_router, tm=256):
    T, D = x.shape
    E = w_router.shape[1]
    w = jnp.pad(w_router, ((0, 0), (0, LANES - E)))
    idx, prob = pl.pallas_call(
        functools.partial(_router_kernel, n_experts=E),
        grid=(T // tm,),
        in_specs=[pl.BlockSpec((tm, D), lambda i: (i, 0)),
                  pl.BlockSpec((D, LANES), lambda i: (0, 0))],
        out_specs=[pl.BlockSpec((tm, LANES), lambda i: (i, 0))] * 2,
        out_shape=[jax.ShapeDtypeStruct((T, LANES), jnp.int32),
                   jax.ShapeDtypeStruct((T, LANES), F32)],
        compiler_params=_params("parallel"),
        name="router",
    )(x, w)
    return idx[:, :TOP_K], prob[:, :TOP_K]


def _route_tables(top_i, top_p, tm, n_tiles):
    T = top_i.shape[0]
    A = T * TOP_K
    flat_e = top_i.reshape(A)
    flat_p = top_p.reshape(A)
    order = jnp.argsort(flat_e, stable=True).astype(jnp.int32)
    counts = jnp.sum(flat_e[:, None] == jnp.arange(N_EXPERTS, dtype=jnp.int32)[None, :], axis=0,
                     dtype=jnp.int32)
    tiles_per = (counts + tm - 1) // tm
    tile_end = jnp.cumsum(tiles_per)
    row_start = (tile_end - tiles_per) * tm
    sorted_start = jnp.cumsum(counts) - counts
    e_sorted = flat_e[order]
    pos_sorted = row_start[e_sorted] + jnp.arange(A, dtype=jnp.int32) - sorted_start[e_sorted]
    P = n_tiles * tm
    row_token = jnp.zeros((P,), jnp.int32).at[pos_sorted].set(order // TOP_K)
    row_scale = jnp.zeros((P,), F32).at[pos_sorted].set(flat_p[order])
    pos = jnp.zeros((A,), jnp.int32).at[order].set(pos_sorted).reshape(T, TOP_K)
    tile_ids = jnp.arange(n_tiles, dtype=jnp.int32)
    used = tile_end[-1]
    tile_valid = (tile_ids < used).astype(jnp.int32)
    te = jnp.sum(tile_ids[:, None] >= tile_end[None, :], axis=1, dtype=jnp.int32)
    te = jnp.minimum(te, N_EXPERTS - 1)
    last = jnp.sum((used - 1) >= tile_end, dtype=jnp.int32)
    tile_expert = jnp.where(tile_valid != 0, te, jnp.minimum(last, N_EXPERTS - 1))
    return row_token, row_scale.reshape(P, 1), pos, tile_expert, tile_valid


def _gather_kernel(tok_ref, x_hbm, o_ref, buf, sem):
    tm = buf.shape[0]
    base = pl.program_id(0) * tm

    def row_copy(r):
        return pltpu.make_async_copy(x_hbm.at[pl.ds(tok_ref[base + r], 1)], buf.at[pl.ds(r, 1)], sem)

    def start(r, c):
        row_copy(r).start()
        return c

    def wait(r, c):
        row_copy(r).wait()
        return c

    lax.fori_loop(0, tm, start, 0)
    lax.fori_loop(0, tm, wait, 0)
    o_ref[...] = buf[...].astype(o_ref.dtype)


def _gather_rows(x, row_token, tm=256):
    P = row_token.shape[0]
    D = x.shape[1]
    grid_spec = pltpu.PrefetchScalarGridSpec(
        num_scalar_prefetch=1,
        grid=(P // tm,),
        in_specs=[pl.BlockSpec(memory_space=pl.ANY)],
        out_specs=pl.BlockSpec((tm, D), lambda t, tok: (t, 0)),
        scratch_shapes=[pltpu.VMEM((tm, D), F32), pltpu.SemaphoreType.DMA(())],
    )
    return pl.pallas_call(
        _gather_kernel,
        grid_spec=grid_spec,
        out_shape=jax.ShapeDtypeStruct((P, D), BF16),
        compiler_params=_params("arbitrary"),
        name="gather_rows",
    )(row_token, x)


def _combine_kernel(p0_ref, p1_ref, y_hbm, o_ref, buf0, buf1, sem):
    tm = buf0.shape[0]
    base = pl.program_id(0) * tm

    def copies(r):
        return (pltpu.make_async_copy(y_hbm.at[pl.ds(p0_ref[base + r], 1)], buf0.at[pl.ds(r, 1)], sem),
                pltpu.make_async_copy(y_hbm.at[pl.ds(p1_ref[base + r], 1)], buf1.at[pl.ds(r, 1)], sem))

    def start(r, c):
        for cp in copies(r):
            cp.start()
        return c

    def wait(r, c):
        for cp in copies(r):
            cp.wait()
        return c

    lax.fori_loop(0, tm, start, 0)
    lax.fori_loop(0, tm, wait, 0)
    o_ref[...] = buf0[...] + buf1[...]


def _combine_rows(y, pos, tm=256):
    T = pos.shape[0]
    D = y.shape[1]
    grid_spec = pltpu.PrefetchScalarGridSpec(
        num_scalar_prefetch=2,
        grid=(T // tm,),
        in_specs=[pl.BlockSpec(memory_space=pl.ANY)],
        out_specs=pl.BlockSpec((tm, D), lambda t, p0, p1: (t, 0)),
        scratch_shapes=[pltpu.VMEM((tm, D), F32), pltpu.VMEM((tm, D), F32),
                        pltpu.SemaphoreType.DMA(())],
    )
    return pl.pallas_call(
        _combine_kernel,
        grid_spec=grid_spec,
        out_shape=jax.ShapeDtypeStruct((T, D), F32),
        compiler_params=_params("arbitrary"),
        name="combine_rows",
    )(pos[:, 0], pos[:, 1], y)


def _ple_kernel(h_ref, d_ref, p_ref, g_ref, wgd_ref, wgu_ref, wup_ref, *rest, n_norm):
    gn_refs = rest[:n_norm]
    o_ref = rest[n_norm]
    on_refs = rest[n_norm + 1:]
    h = h_ref[...] + d_ref[...]
    y = h * lax.rsqrt(jnp.mean(h * h, axis=-1, keepdims=True) + EPS) * g_ref[...]
    t = jnp.dot(y.astype(BF16), wgd_ref[...], preferred_element_type=F32)
    gate = _sigmoid(jnp.dot(t.astype(BF16), wgu_ref[...], preferred_element_type=F32))
    up = jnp.dot(p_ref[...].astype(BF16), wup_ref[...], preferred_element_type=F32)
    h = h + gate * up
    o_ref[...] = h
    if n_norm:
        y = h * lax.rsqrt(jnp.mean(h * h, axis=-1, keepdims=True) + EPS)
        for gn_ref, on_ref in zip(gn_refs, on_refs):
            on_ref[...] = (y * gn_ref[...]).astype(on_ref.dtype)


def _ple(h, delta, p_i, g_norm, w_up, w_gdown, w_gup, next_gains, tm=256):
    T, D = h.shape
    R = p_i.shape[1]
    row = pl.BlockSpec((tm, D), lambda i: (i, 0))
    gain = pl.BlockSpec((1, D), lambda i: (0, 0))
    n_norm = len(next_gains)
    outs = pl.pallas_call(
        functools.partial(_ple_kernel, n_norm=n_norm),
        grid=(T // tm,),
        in_specs=[row, row, pl.BlockSpec((tm, R), lambda i: (i, 0)), gain,
                  pl.BlockSpec((D, R), lambda i: (0, 0)),
                  pl.BlockSpec((R, D), lambda i: (0, 0)),
                  pl.BlockSpec((R, D), lambda i: (0, 0))] + [gain] * n_norm,
        out_specs=[row] * (1 + n_norm),
        out_shape=[jax.ShapeDtypeStruct((T, D), F32)] + [jax.ShapeDtypeStruct((T, D), BF16)] * n_norm,
        compiler_params=_params("parallel"),
        name="per_layer_embedding",
    )(h, delta, p_i, g_norm.reshape(1, D), w_gdown.astype(BF16), w_gup.astype(BF16),
      w_up.astype(BF16), *[g.reshape(1, D) for g in next_gains])
    return list(outs)


def kernel(x, p, norm_mix, norm_ffn, norm_ple, a_w_qkv, a_q_norm, a_k_norm, a_w_o, kv_norm, w_kv, b_w_q, b_w_o, ffn_w_gate, ffn_w_up, ffn_w_down, moe_w_router, moe_w_gate, moe_w_up, moe_w_down, ple_w_up, ple_w_gdown, ple_w_gup):
    B, S, D = x.shape
    T = B * S
    assert p.shape[0] == 2 and a_w_qkv.shape[0] == 1 and b_w_q.shape[0] == 1
    G = len(A_GROUPS)
    a_heads = a_w_o.shape[1] // HEAD_DIM
    b_heads = b_w_q.shape[2] // HEAD_DIM
    x2 = x.reshape(T, D)
    p2 = p.reshape(2, T, p.shape[-1])

    (hn,) = _rmsnorm(x2, None, [norm_mix[0]], [BF16])
    qkv = _qkv_proj(hn, a_w_qkv[0], a_q_norm[0], a_k_norm[0], S)
    qkv = qkv.reshape(B, S, 3, G, a_heads * HEAD_DIM)
    os, sts = [], []
    for g, (span, dil) in enumerate(A_GROUPS):
        assert span // dil == ATTN_BLOCK
        o_g, st_g = _dilated_attention_group(qkv[:, :, :, g], B, S, dil, a_heads)
        os.append(o_g)
        sts.append(st_g)
    attn = _merge_groups(os, sts, a_heads)
    h = _matmul(attn, a_w_o[0], F32, residual=x2, tn=1024)

    (hn,) = _rmsnorm(h, None, [norm_ffn[0]], [BF16])
    n_tiles = T // 1024
    delta = _swiglu(hn, ffn_w_gate, ffn_w_up, ffn_w_down,
                    jnp.zeros((n_tiles,), jnp.int32), jnp.ones((n_tiles,), jnp.int32), None)
    h, hn_kv, hn_q = _ple(h, delta, p2[0], norm_ple[0], ple_w_up[0], ple_w_gdown[0], ple_w_gup[0],
                          [kv_norm, norm_mix[1]])

    kv = _matmul(hn_kv, w_kv, BF16)
    q = _matmul(hn_q, b_w_q[0], BF16)
    sb = _stick_breaking(q, kv, B, S, b_heads)
    h = _matmul(sb, b_w_o[0], F32, residual=h)

    (hn32,) = _rmsnorm(h, None, [norm_ffn[1]], [F32])
    top_i, top_p = _router(hn32, moe_w_router[0])
    tm = 1024
    n_tiles = (T * TOP_K) // tm + N_EXPERTS
    row_token, row_scale, pos, tile_expert, tile_valid = _route_tables(top_i, top_p, tm, n_tiles)
    xs = _gather_rows(hn32, row_token)
    ys = _swiglu(xs, moe_w_gate[0], moe_w_up[0], moe_w_down[0], tile_expert, tile_valid, row_scale, tm=tm)
    delta = _combine_rows(ys, pos)
    (h,) = _ple(h, delta, p2[1], norm_ple[1], ple_w_up[1], ple_w_gdown[1], ple_w_gup[1], [])
    return h.reshape(B, S, D)
```

```python
import functools
import math

import jax
import jax.numpy as jnp
from jax import lax
from jax.experimental import pallas as pl
from jax.experimental.pallas import tpu as pltpu

F32 = jnp.float32
BF16 = jnp.bfloat16

EPS = 1e-6
HEAD_DIM = 128
ROPE_THETA = 10000.0
A_GROUPS = ((128, 1), (512, 4), (2048, 16))
N_EXPERTS = 8
TOP_K = 2

LANES = 128
ATTN_BLOCK = 128
SB_BLOCK = 256
SB_UNROLL = 4
DOWN_CHUNK = 512
VMEM_LIMIT = 56 * 1024 * 1024

_NT = (((1,), (1,)), ((), ()))


def _params(*semantics):
    return pltpu.CompilerParams(dimension_semantics=semantics, vmem_limit_bytes=VMEM_LIMIT)


def _sigmoid(x):
    return 1.0 / (1.0 + jnp.exp(-x))


def _residue_perm(n_rows, dil, inverse):
    n = n_rows // dil
    assert dil & (dil - 1) == 0 and n & (n - 1) == 0
    out_row = lax.broadcasted_iota(jnp.int32, (n_rows, n_rows), 0)
    in_row = lax.broadcasted_iota(jnp.int32, (n_rows, n_rows), 1)
    if inverse:
        src = (out_row & (dil - 1)) * n + (out_row >> (dil.bit_length() - 1))
    else:
        src = (out_row & (n - 1)) * dil + (out_row >> (n.bit_length() - 1))
    return in_row == src


def _norm_kernel(*refs, has_delta, n_out):
    refs = list(refs)
    h_ref = refs.pop(0)
    d_ref = refs.pop(0) if has_delta else None
    g_refs = [refs.pop(0) for _ in range(n_out)]
    hsum_ref = refs.pop(0) if has_delta else None
    o_refs = refs
    h = h_ref[...]
    if has_delta:
        h = h + d_ref[...]
        hsum_ref[...] = h
    y = h * lax.rsqrt(jnp.mean(h * h, axis=-1, keepdims=True) + EPS)
    for g_ref, o_ref in zip(g_refs, o_refs):
        o_ref[...] = (y * g_ref[...]).astype(o_ref.dtype)


def _rmsnorm(h, delta, gains, out_dtypes, tm=256):
    T, D = h.shape
    has_delta = delta is not None
    row = pl.BlockSpec((tm, D), lambda i: (i, 0))
    gain = pl.BlockSpec((1, D), lambda i: (0, 0))
    ins = [h] + ([delta] if has_delta else []) + [g.reshape(1, D) for g in gains]
    in_specs = [row] * (2 if has_delta else 1) + [gain] * len(gains)
    out_shape = ([jax.ShapeDtypeStruct((T, D), F32)] if has_delta else []) + [
        jax.ShapeDtypeStruct((T, D), dt) for dt in out_dtypes]
    outs = pl.pallas_call(
        functools.partial(_norm_kernel, has_delta=has_delta, n_out=len(gains)),
        grid=(T // tm,),
        in_specs=in_specs,
        out_specs=[row] * len(out_shape),
        out_shape=out_shape,
        compiler_params=_params("parallel"),
        name="rmsnorm",
    )(*ins)
    return list(outs)


def _norm_dilated_kernel(h_ref, g_ref, *o_refs, dils):
    h = h_ref[...]
    tm = h.shape[0]
    y = (h * lax.rsqrt(jnp.mean(h * h, axis=-1, keepdims=True) + EPS) * g_ref[...]).astype(BF16)
    for o_ref, dil in zip(o_refs, dils):
        if dil == 1:
            o_ref[0, 0] = y
        else:
            perm = _residue_perm(tm, dil, inverse=False).astype(BF16)
            yp = jnp.dot(perm, y, preferred_element_type=F32).astype(BF16)
            n = tm // dil
            for r in range(dil):
                o_ref[0, r] = yp[r * n:(r + 1) * n]


def _rmsnorm_dilated(h, gain, B, S, dils, tm=256):
    T, D = h.shape
    bpb = S // tm
    return pl.pallas_call(
        functools.partial(_norm_dilated_kernel, dils=tuple(dils)),
        grid=(T // tm,),
        in_specs=[pl.BlockSpec((tm, D), lambda i: (i, 0)), pl.BlockSpec((1, D), lambda i: (0, 0))],
        out_specs=[pl.BlockSpec((1, d, tm // d, D), lambda i: (i // bpb, 0, i % bpb, 0)) for d in dils],
        out_shape=[jax.ShapeDtypeStruct((B, d, S // d, D), BF16) for d in dils],
        compiler_params=_params("parallel"),
        name="rmsnorm_dilated",
    )(h, gain.reshape(1, D))


def _mm_kernel(x_ref, w_ref, *rest, has_res):
    acc = jnp.dot(x_ref[...], w_ref[...].astype(BF16), preferred_element_type=F32)
    if has_res:
        r_ref, o_ref = rest
        acc = r_ref[...] + acc
    else:
        (o_ref,) = rest
    o_ref[...] = acc.astype(o_ref.dtype)


def _matmul(x, w, out_dtype, residual=None, tm=1024, tn=512):
    M, K = x.shape
    N = w.shape[1]
    has_res = residual is not None
    in_specs = [pl.BlockSpec((tm, K), lambda i, j: (i, 0)),
                pl.BlockSpec((K, tn), lambda i, j: (0, j))]
    ins = [x, w]
    if has_res:
        in_specs.append(pl.BlockSpec((tm, tn), lambda i, j: (i, j)))
        ins.append(residual)
    return pl.pallas_call(
        functools.partial(_mm_kernel, has_res=has_res),
        grid=(M // tm, N // tn),
        in_specs=in_specs,
        out_specs=pl.BlockSpec((tm, tn), lambda i, j: (i, j)),
        out_shape=jax.ShapeDtypeStruct((M, N), out_dtype),
        compiler_params=_params("parallel", "arbitrary"),
        name="matmul",
    )(*ins)


def _qkv_kernel(x_ref, w_ref, cos_ref, sin_ref, gq_ref, gk_ref, o_ref, *, q_tiles, qk_tiles):
    j = pl.program_id(1)
    acc = jnp.dot(x_ref[...], w_ref[...].astype(BF16), preferred_element_type=F32)

    @pl.when(j < qk_tiles)
    def _():
        gain = jnp.where(j < q_tiles, gq_ref[...], gk_ref[...])
        cos = cos_ref[...]
        sin = sin_ref[...]
        for hh in range(acc.shape[1] // HEAD_DIM):
            sl = slice(hh * HEAD_DIM, (hh + 1) * HEAD_DIM)
            blk = acc[:, sl]
            y = blk * lax.rsqrt(jnp.mean(blk * blk, axis=-1, keepdims=True) + EPS) * gain
            o_ref[:, sl] = (y * cos + pltpu.roll(y, HEAD_DIM // 2, 1) * sin).astype(o_ref.dtype)

    @pl.when(j >= qk_tiles)
    def _():
        o_ref[...] = acc.astype(o_ref.dtype)


def _rope_tables(S, dil):
    half = HEAD_DIM // 2
    inv = ROPE_THETA ** (-jnp.arange(half, dtype=F32) / half)
    pos = jnp.arange(S, dtype=F32).reshape(S // dil, dil).T.reshape(S)
    ang = pos[:, None] * inv[None, :]
    cos, sin = jnp.cos(ang), jnp.sin(ang)
    return jnp.concatenate([cos, cos], axis=-1), jnp.concatenate([-sin, sin], axis=-1)


def _qkv_proj(x, w, g_q, g_k, S, dil, group, n_groups, tm=1024, tn=512):
    M, K = x.shape
    hw = w.shape[1] // (3 * n_groups)
    tpg = hw // tn
    cos, sin = _rope_tables(S, dil)
    pos_blocks = S // tm
    w_col = lambda i, j: (0, ((j // tpg) * n_groups + group) * tpg + j % tpg)
    return pl.pallas_call(
        functools.partial(_qkv_kernel, q_tiles=tpg, qk_tiles=2 * tpg),
        grid=(M // tm, 3 * tpg),
        in_specs=[pl.BlockSpec((tm, K), lambda i, j: (i, 0)),
                  pl.BlockSpec((K, tn), w_col),
                  pl.BlockSpec((tm, HEAD_DIM), lambda i, j: (i % pos_blocks, 0)),
                  pl.BlockSpec((tm, HEAD_DIM), lambda i, j: (i % pos_blocks, 0)),
                  pl.BlockSpec((1, HEAD_DIM), lambda i, j: (0, 0)),
                  pl.BlockSpec((1, HEAD_DIM), lambda i, j: (0, 0))],
        out_specs=pl.BlockSpec((tm, tn), lambda i, j: (i, j)),
        out_shape=jax.ShapeDtypeStruct((M, 3 * hw), BF16),
        compiler_params=_params("parallel", "arbitrary"),
        name=f"qkv_proj_d{dil}",
    )(x, w, cos, sin, g_q.reshape(1, HEAD_DIM), g_k.reshape(1, HEAD_DIM))


def _dil_attn_kernel(q_ref, kp_ref, kc_ref, vp_ref, vc_ref, o_ref, st_ref, *, n_heads, scale):
    n = pl.program_id(2)
    blk = q_ref.shape[0]
    i = lax.broadcasted_iota(jnp.int32, (blk, blk), 0)
    j = lax.broadcasted_iota(jnp.int32, (blk, blk), 1)
    mask_p = (j >= i) & (n > 0)
    mask_c = j <= i
    lane = lax.broadcasted_iota(jnp.int32, (blk, LANES), 1)
    stats = jnp.zeros((blk, LANES), F32)
    for h in range(n_heads):
        sl = slice(h * HEAD_DIM, (h + 1) * HEAD_DIM)
        q = q_ref[:, sl]
        s_p = lax.dot_general(q, kp_ref[:, sl], _NT, preferred_element_type=F32) * scale
        s_c = lax.dot_general(q, kc_ref[:, sl], _NT, preferred_element_type=F32) * scale
        s_p = jnp.where(mask_p, s_p, -jnp.inf)
        s_c = jnp.where(mask_c, s_c, -jnp.inf)
        m = jnp.maximum(jnp.max(s_p, axis=-1, keepdims=True), jnp.max(s_c, axis=-1, keepdims=True))
        p_p = jnp.exp(s_p - m)
        p_c = jnp.exp(s_c - m)
        l = jnp.sum(p_p, axis=-1, keepdims=True) + jnp.sum(p_c, axis=-1, keepdims=True)
        o = (jnp.dot(p_p.astype(BF16), vp_ref[:, sl], preferred_element_type=F32)
             + jnp.dot(p_c.astype(BF16), vc_ref[:, sl], preferred_element_type=F32))
        o_ref[:, sl] = (o / l).astype(o_ref.dtype)
        stats = jnp.where(lane == h, m + jnp.log(l), stats)
    st_ref[...] = stats


def _dilated_attention_group(qkv_g, B, S, dil, n_heads):
    hw = n_heads * HEAD_DIM
    L = S // dil
    nb = L // ATTN_BLOCK
    a = qkv_g.reshape(B, dil, L, 3 * hw)
    blk = (None, None, ATTN_BLOCK, hw)
    cur = lambda c: (lambda b, r, n: (b, r, n, c))
    prev = lambda c: (lambda b, r, n: (b, r, jnp.maximum(n - 1, 0), c))
    return pl.pallas_call(
        functools.partial(_dil_attn_kernel, n_heads=n_heads, scale=1.0 / math.sqrt(HEAD_DIM)),
        grid=(B, dil, nb),
        in_specs=[pl.BlockSpec(blk, cur(0)),
                  pl.BlockSpec(blk, prev(1)), pl.BlockSpec(blk, cur(1)),
                  pl.BlockSpec(blk, prev(2)), pl.BlockSpec(blk, cur(2))],
        out_specs=[pl.BlockSpec(blk, lambda b, r, n: (b, r, n, 0)),
                   pl.BlockSpec((None, None, ATTN_BLOCK, LANES), lambda b, r, n: (b, r, n, 0))],
        out_shape=[jax.ShapeDtypeStruct((B, dil, L, hw), BF16),
                   jax.ShapeDtypeStruct((B, dil, L, LANES), F32)],
        compiler_params=_params("parallel", "parallel", "arbitrary"),
        name=f"dilated_attn_d{dil}",
    )(a, a, a, a, a)


def _merge_kernel(*refs, dils, n_heads):
    G = len(dils)
    o_refs, s_refs, out_ref = refs[:G], refs[G:2 * G], refs[2 * G]
    tm, hw = out_ref.shape
    o, s = [], []
    for o_ref, s_ref, dil in zip(o_refs, s_refs, dils):
        o_g = o_ref[...].reshape(tm, hw)
        s_g = s_ref[...].reshape(tm, LANES)
        if dil == 1:
            o_g = o_g.astype(F32)
        else:
            pinv = _residue_perm(tm, dil, inverse=True)
            o_g = jnp.dot(pinv.astype(BF16), o_g, preferred_element_type=F32)
            s_g = jnp.dot(pinv.astype(F32), s_g, preferred_element_type=F32,
                          precision=lax.Precision.HIGHEST)
        o.append(o_g)
        s.append(s_g)
    m = functools.reduce(jnp.maximum, s)
    e = [jnp.exp(x - m) for x in s]
    den = functools.reduce(lambda a, b: a + b, e)
    w = [x / den for x in e]
    for h in range(n_heads):
        sl = slice(h * HEAD_DIM, (h + 1) * HEAD_DIM)
        acc = w[0][:, h:h + 1] * o[0][:, sl]
        for g in range(1, G):
            acc = acc + w[g][:, h:h + 1] * o[g][:, sl]
        out_ref[:, sl] = acc.astype(out_ref.dtype)


def _merge_groups(os, sts, dils, B, S, n_heads, tm=256):
    hw = n_heads * HEAD_DIM
    bpb = S // tm
    spec = lambda d, w: pl.BlockSpec((None, d, tm // d, w), lambda i: (i // bpb, 0, i % bpb, 0))
    return pl.pallas_call(
        functools.partial(_merge_kernel, dils=tuple(dils), n_heads=n_heads),
        grid=(B * bpb,),
        in_specs=[spec(d, hw) for d in dils] + [spec(d, LANES) for d in dils],
        out_specs=pl.BlockSpec((tm, hw), lambda i: (i, 0)),
        out_shape=jax.ShapeDtypeStruct((B * S, hw), BF16),
        compiler_params=_params("parallel"),
        name="merge_groups",
    )(*os, *sts)


def _sb_kernel(q_ref, k_ref, v_ref, o_ref, *, scale):
    qi = pl.program_id(2)
    tq = q_ref.shape[1]
    q = q_ref[0]
    row = lax.broadcasted_iota(jnp.int32, (tq, tq), 0)
    col = lax.broadcasted_iota(jnp.int32, (tq, tq), 1)
    strict = col < row
    tri = (row > col).astype(BF16)

    def tiles(first, n, carry, diag_first):
        acc, run = carry
        starts = [pl.multiple_of((first - u) * tq, tq) for u in range(n)]
        zs = [lax.dot_general(q, k_ref[0, pl.ds(s, tq), :], _NT, preferred_element_type=F32) * scale
              for s in starts]
        lks, lss = [], []
        for u, z in enumerate(zs):
            e = jnp.log(1.0 + jnp.exp(-jnp.abs(z)))
            lk = -(jnp.maximum(z, 0.0) + e)
            if diag_first and u == 0:
                lk = jnp.where(strict, lk, 0.0)
            lks.append(lk)
            lss.append(jnp.minimum(z, 0.0) - e)
        his = [lk.astype(BF16) for lk in lks]
        los = [(lk - hi.astype(F32)).astype(BF16) for lk, hi in zip(lks, his)]
        sufs = [jnp.dot(hi, tri, preferred_element_type=F32) + jnp.dot(lo, tri, preferred_element_type=F32)
                for hi, lo in zip(his, los)]
        ps = []
        for u in range(n):
            a = jnp.exp(lss[u] + (sufs[u] + run))
            if diag_first and u == 0:
                a = jnp.where(strict, a, 0.0)
            ps.append(a.astype(BF16))
            run = run + jnp.sum(lks[u], axis=1, keepdims=True)
        for u in range(n):
            acc = acc + jnp.dot(ps[u], v_ref[0, pl.ds(starts[u], tq), :], preferred_element_type=F32)
        return acc, run

    carry = (jnp.zeros((tq, HEAD_DIM), F32), jnp.zeros((tq, 1), F32))
    rem = qi % SB_UNROLL
    carry = lax.switch(rem, [functools.partial(tiles, qi, 1 + r, diag_first=True)
                             for r in range(SB_UNROLL)], carry)
    carry = lax.fori_loop(
        0, qi // SB_UNROLL,
        lambda t, c: tiles(qi - 1 - rem - t * SB_UNROLL, SB_UNROLL, c, False), carry)
    o_ref[0] = carry[0].astype(o_ref.dtype)


def _stick_breaking(q, kv, B, S, n_heads):
    hw = n_heads * HEAD_DIM
    q3 = q.reshape(B, S, hw)
    kv3 = kv.reshape(B, S, 2 * hw)
    tq = SB_BLOCK
    out = pl.pallas_call(
        functools.partial(_sb_kernel, scale=1.0 / math.sqrt(HEAD_DIM)),
        grid=(B, n_heads, S // tq),
        in_specs=[pl.BlockSpec((1, tq, HEAD_DIM), lambda b, h, i: (b, i, h)),
                  pl.BlockSpec((1, S, HEAD_DIM), lambda b, h, i: (b, 0, h)),
                  pl.BlockSpec((1, S, HEAD_DIM), lambda b, h, i: (b, 0, n_heads + h))],
        out_specs=pl.BlockSpec((1, tq, HEAD_DIM), lambda b, h, i: (b, i, h)),
        out_shape=jax.ShapeDtypeStruct((B, S, hw), BF16),
        compiler_params=_params("parallel", "parallel", "arbitrary"),
        name="stick_breaking",
    )(q3, kv3, kv3)
    return out.reshape(B * S, hw)


def _swiglu_kernel(te_ref, tv_ref, x_ref, wg_ref, wu_ref, wd_ref, sc_ref, o_ref, *, use_scale):
    t = pl.program_id(0)
    f = pl.program_id(1)
    valid = tv_ref[t] != 0

    @pl.when(valid)
    def _():
        x = x_ref[...]
        g = jnp.dot(x, wg_ref[...].astype(BF16), preferred_element_type=F32)
        u = jnp.dot(x, wu_ref[...].astype(BF16), preferred_element_type=F32)
        mid = (g * _sigmoid(g) * u).astype(BF16)

        @pl.when(f == 0)
        def _():
            o_ref[...] = jnp.zeros_like(o_ref)

        for c in range(0, o_ref.shape[1], DOWN_CHUNK):
            sl = slice(c, c + DOWN_CHUNK)
            o_ref[:, sl] += jnp.dot(mid, wd_ref[:, sl].astype(BF16), preferred_element_type=F32)

        if use_scale:
            @pl.when(f == pl.num_programs(1) - 1)
            def _():
                o_ref[...] *= sc_ref[...]

    @pl.when(jnp.logical_not(valid) & (f == 0))
    def _():
        o_ref[...] = jnp.zeros_like(o_ref)


def _swiglu(x, w_gate, w_up, w_down, tile_expert, tile_valid, row_scale, tm=1024, tf=256):
    P, D = x.shape
    E, _, F = w_gate.shape
    nf = F // tf
    use_scale = row_scale is not None
    if not use_scale:
        row_scale = jnp.ones((P, 1), F32)
    fidx = lambda t, f, tv: jnp.where(tv[t] != 0, f, nf - 1)
    grid_spec = pltpu.PrefetchScalarGridSpec(
        num_scalar_prefetch=2,
        grid=(P // tm, nf),
        in_specs=[pl.BlockSpec((tm, D), lambda t, f, te, tv: (t, 0), pipeline_mode=pl.Buffered(1)),
                  pl.BlockSpec((None, D, tf), lambda t, f, te, tv: (te[t], 0, fidx(t, f, tv))),
                  pl.BlockSpec((None, D, tf), lambda t, f, te, tv: (te[t], 0, fidx(t, f, tv))),
                  pl.BlockSpec((None, tf, D), lambda t, f, te, tv: (te[t], fidx(t, f, tv), 0)),
                  pl.BlockSpec((tm, 1), lambda t, f, te, tv: (t, 0))],
        out_specs=pl.BlockSpec((tm, D), lambda t, f, te, tv: (t, 0), pipeline_mode=pl.Buffered(1)),
    )
    return pl.pallas_call(
        functools.partial(_swiglu_kernel, use_scale=use_scale),
        grid_spec=grid_spec,
        out_shape=jax.ShapeDtypeStruct((P, D), F32),
        compiler_params=_params("parallel", "arbitrary"),
        name="swiglu",
    )(tile_expert, tile_valid, x, w_gate, w_up, w_down, row_scale)


def _router_kernel(x_ref, w_ref, idx_ref, p_ref, *, n_experts):
    logits = jnp.dot(x_ref[...], w_ref[...], preferred_element_type=F32,
                     precision=lax.Precision.HIGHEST)
    lane = lax.broadcasted_iota(jnp.int32, logits.shape, 1)
    logits = jnp.where(lane < n_experts, logits, -jnp.inf)
    m1 = jnp.max(logits, axis=-1, keepdims=True)
    i1 = jnp.min(jnp.where(logits == m1, lane, LANES), axis=-1, keepdims=True)
    rest = jnp.where(lane == i1, -jnp.inf, logits)
    m2 = jnp.max(rest, axis=-1, keepdims=True)
    i2 = jnp.min(jnp.where(rest == m2, lane, LANES), axis=-1, keepdims=True)
    e = jnp.exp(m2 - m1)
    p1 = 1.0 / (1.0 + e)
    p2 = e / (1.0 + e)
    idx_ref[...] = jnp.where(lane == 0, i1, jnp.where(lane == 1, i2, 0))
    p_ref[...] = jnp.where(lane == 0, p1, jnp.where(lane == 1, p2, 0.0))


def _router(x, w_router, tm=256):
    T, D = x.shape
    E = w_router.shape[1]
    w = jnp.pad(w_router, ((0, 0), (0, LANES - E)))
    idx, prob = pl.pallas_call(
        functools.partial(_router_kernel, n_experts=E),
        grid=(T // tm,),
        in_specs=[pl.BlockSpec((tm, D), lambda i: (i, 0)),
                  pl.BlockSpec((D, LANES), lambda i: (0, 0))],
        out_specs=[pl.BlockSpec((tm, LANES), lambda i: (i, 0))] * 2,
        out_shape=[jax.ShapeDtypeStruct((T, LANES), jnp.int32),
                   jax.ShapeDtypeStruct((T, LANES), F32)],
        compiler_params=_params("parallel"),
        name="router",
    )(x, w)
    return idx[:, :TOP_K], prob[:, :TOP_K]


def _route_tables(top_i, top_p, tm, n_tiles):
    T = top_i.shape[0]
    A = T * TOP_K
    flat_e = top_i.reshape(A)
    flat_p = top_p.reshape(A)
    order = jnp.argsort(flat_e, stable=True).astype(jnp.int32)
    counts = jnp.sum(flat_e[:, None] == jnp.arange(N_EXPERTS, dtype=jnp.int32)[None, :], axis=0,
                     dtype=jnp.int32)
    tiles_per = (counts + tm - 1) // tm
    tile_end = jnp.cumsum(tiles_per)
    row_start = (tile_end - tiles_per) * tm
    sorted_start = jnp.cumsum(counts) - counts
    e_sorted = flat_e[order]
    pos_sorted = row_start[e_sorted] + jnp.arange(A, dtype=jnp.int32) - sorted_start[e_sorted]
    P = n_tiles * tm
    row_token = jnp.zeros((P,), jnp.int32).at[pos_sorted].set(order // TOP_K)
    row_scale = jnp.zeros((P,), F32).at[pos_sorted].set(flat_p[order])
    pos = jnp.zeros((A,), jnp.int32).at[order].set(pos_sorted).reshape(T, TOP_K)
    tile_ids = jnp.arange(n_tiles, dtype=jnp.int32)
    used = tile_end[-1]
    tile_valid = (tile_ids < used).astype(jnp.int32)
    te = jnp.sum(tile_ids[:, None] >= tile_end[None, :], axis=1, dtype=jnp.int32)
    te = jnp.minimum(te, N_EXPERTS - 1)
    last = jnp.sum((used - 1) >= tile_end, dtype=jnp.int32)
    tile_expert = jnp.where(tile_valid != 0, te, jnp.minimum(last, N_EXPERTS - 1))
    return row_token, row_scale.reshape(P, 1), pos, tile_expert, tile_valid


def _gather_kernel(tok_ref, x_hbm, o_ref, buf, sem):
    tm = buf.shape[1]
    t = pl.program_id(0)

    def row_copy(tile, r):
        slot = tile % 2
        return pltpu.make_async_copy(x_hbm.at[pl.ds(tok_ref[tile * tm + r], 1)],
                                     buf.at[slot, pl.ds(r, 1)], sem.at[slot])

    def start_tile(tile):
        def body(r, c):
            row_copy(tile, r).start()
            return c
        lax.fori_loop(0, tm, body, 0, unroll=8)

    @pl.when(t == 0)
    def _():
        start_tile(0)

    @pl.when(t + 1 < pl.num_programs(0))
    def _():
        start_tile(t + 1)

    def wait(r, c):
        row_copy(t, r).wait()
        return c

    lax.fori_loop(0, tm, wait, 0, unroll=8)
    o_ref[...] = buf[t % 2].astype(o_ref.dtype)


def _gather_rows(x, row_token, tm=256):
    P = row_token.shape[0]
    D = x.shape[1]
    grid_spec = pltpu.PrefetchScalarGridSpec(
        num_scalar_prefetch=1,
        grid=(P // tm,),
        in_specs=[pl.BlockSpec(memory_space=pl.ANY)],
        out_specs=pl.BlockSpec((tm, D), lambda t, tok: (t, 0)),
        scratch_shapes=[pltpu.VMEM((2, tm, D), F32), pltpu.SemaphoreType.DMA((2,))],
    )
    return pl.pallas_call(
        _gather_kernel,
        grid_spec=grid_spec,
        out_shape=jax.ShapeDtypeStruct((P, D), BF16),
        compiler_params=_params("arbitrary"),
        name="gather_rows",
    )(row_token, x)


def _combine_kernel(p0_ref, p1_ref, y_hbm, o_ref, buf0, buf1, sem):
    tm = buf0.shape[0]
    base = pl.program_id(0) * tm

    def copies(r):
        return (pltpu.make_async_copy(y_hbm.at[pl.ds(p0_ref[base + r], 1)], buf0.at[pl.ds(r, 1)], sem),
                pltpu.make_async_copy(y_hbm.at[pl.ds(p1_ref[base + r], 1)], buf1.at[pl.ds(r, 1)], sem))

    def start(r, c):
        for cp in copies(r):
            cp.start()
        return c

    def wait(r, c):
        for cp in copies(r):
            cp.wait()
        return c

    lax.fori_loop(0, tm, start, 0)
    lax.fori_loop(0, tm, wait, 0)
    o_ref[...] = buf0[...] + buf1[...]


def _combine_rows(y, pos, tm=256):
    T = pos.shape[0]
    D = y.shape[1]
    grid_spec = pltpu.PrefetchScalarGridSpec(
        num_scalar_prefetch=2,
        grid=(T // tm,),
        in_specs=[pl.BlockSpec(memory_space=pl.ANY)],
        out_specs=pl.BlockSpec((tm, D), lambda t, p0, p1: (t, 0)),
        scratch_shapes=[pltpu.VMEM((tm, D), F32), pltpu.VMEM((tm, D), F32),
                        pltpu.SemaphoreType.DMA(())],
    )
    return pl.pallas_call(
        _combine_kernel,
        grid_spec=grid_spec,
        out_shape=jax.ShapeDtypeStruct((T, D), F32),
        compiler_params=_params("arbitrary"),
        name="combine_rows",
    )(pos[:, 0], pos[:, 1], y)


def _ple_kernel(h_ref, d_ref, p_ref, g_ref, wgd_ref, wgu_ref, wup_ref, *rest, n_norm):
    gn_refs = rest[:n_norm]
    o_ref = rest[n_norm]
    on_refs = rest[n_norm + 1:]
    h = h_ref[...] + d_ref[...]
    y = h * lax.rsqrt(jnp.mean(h * h, axis=-1, keepdims=True) + EPS) * g_ref[...]
    t = jnp.dot(y.astype(BF16), wgd_ref[...], preferred_element_type=F32)
    gate = _sigmoid(jnp.dot(t.astype(BF16), wgu_ref[...], preferred_element_type=F32))
    up = jnp.dot(p_ref[...].astype(BF16), wup_ref[...], preferred_element_type=F32)
    h = h + gate * up
    o_ref[...] = h
    if n_norm:
        y = h * lax.rsqrt(jnp.mean(h * h, axis=-1, keepdims=True) + EPS)
        for gn_ref, on_ref in zip(gn_refs, on_refs):
            on_ref[...] = (y * gn_ref[...]).astype(on_ref.dtype)


def _ple(h, delta, p_i, g_norm, w_up, w_gdown, w_gup, next_gains, tm=256):
    T, D = h.shape
    R = p_i.shape[1]
    row = pl.BlockSpec((tm, D), lambda i: (i, 0))
    gain = pl.BlockSpec((1, D), lambda i: (0, 0))
    n_norm = len(next_gains)
    outs = pl.pallas_call(
        functools.partial(_ple_kernel, n_norm=n_norm),
        grid=(T // tm,),
        in_specs=[row, row, pl.BlockSpec((tm, R), lambda i: (i, 0)), gain,
                  pl.BlockSpec((D, R), lambda i: (0, 0)),
                  pl.BlockSpec((R, D), lambda i: (0, 0)),
                  pl.BlockSpec((R, D), lambda i: (0, 0))] + [gain] * n_norm,
        out_specs=[row] * (1 + n_norm),
        out_shape=[jax.ShapeDtypeStruct((T, D), F32)] + [jax.ShapeDtypeStruct((T, D), BF16)] * n_norm,
        compiler_params=_params("parallel"),
        name="per_layer_embedding",
    )(h, delta, p_i, g_norm.reshape(1, D), w_gdown.astype(BF16), w_gup.astype(BF16),
      w_up.astype(BF16), *[g.reshape(1, D) for g in next_gains])
    return list(outs)


def kernel(x, p, norm_mix, norm_ffn, norm_ple, a_w_qkv, a_q_norm, a_k_norm, a_w_o, kv_norm, w_kv, b_w_q, b_w_o, ffn_w_gate, ffn_w_up, ffn_w_down, moe_w_router, moe_w_gate, moe_w_up, moe_w_down, ple_w_up, ple_w_gdown, ple_w_gup):
    B, S, D = x.shape
    T = B * S
    assert p.shape[0] == 2 and a_w_qkv.shape[0] == 1 and b_w_q.shape[0] == 1
    G = len(A_GROUPS)
    a_heads = a_w_o.shape[1] // HEAD_DIM
    b_heads = b_w_q.shape[2] // HEAD_DIM
    x2 = x.reshape(T, D)
    p2 = p.reshape(2, T, p.shape[-1])

    dils = [dil for _, dil in A_GROUPS]
    hns = _rmsnorm_dilated(x2, norm_mix[0], B, S, dils)
    os, sts = [], []
    for g, (span, dil) in enumerate(A_GROUPS):
        assert span // dil == ATTN_BLOCK
        qkv_g = _qkv_proj(hns[g].reshape(T, D), a_w_qkv[0], a_q_norm[0], a_k_norm[0], S, dil, g, G)
        o_g, st_g = _dilated_attention_group(qkv_g, B, S, dil, a_heads)
        os.append(o_g)
        sts.append(st_g)
    attn = _merge_groups(os, sts, dils, B, S, a_heads)
    h = _matmul(attn, a_w_o[0], F32, residual=x2, tn=1024)

    (hn,) = _rmsnorm(h, None, [norm_ffn[0]], [BF16])
    n_tiles = T // 1024
    delta = _swiglu(hn, ffn_w_gate, ffn_w_up, ffn_w_down,
                    jnp.zeros((n_tiles,), jnp.int32), jnp.ones((n_tiles,), jnp.int32), None)
    h, hn_kv, hn_q = _ple(h, delta, p2[0], norm_ple[0], ple_w_up[0], ple_w_gdown[0], ple_w_gup[0],
                          [kv_norm, norm_mix[1]])

    kv = _matmul(hn_kv, w_kv, BF16)
    q = _matmul(hn_q, b_w_q[0], BF16)
    sb = _stick_breaking(q, kv, B, S, b_heads)
    h = _matmul(sb, b_w_o[0], F32, residual=h)

    (hn32,) = _rmsnorm(h, None, [norm_ffn[1]], [F32])
    top_i, top_p = _router(hn32, moe_w_router[0])
    tm = 1024
    n_tiles = (T * TOP_K) // tm + N_EXPERTS
    row_token, row_scale, pos, tile_expert, tile_valid = _route_tables(top_i, top_p, tm, n_tiles)
    xs = _gather_rows(hn32, row_token)
    ys = _swiglu(xs, moe_w_gate[0], moe_w_up[0], moe_w_down[0], tile_expert, tile_valid, row_scale, tm=tm)
    delta = _combine_rows(ys, pos)
    (h,) = _ple(h, delta, p2[1], norm_ple[1], ple_w_up[1], ple_w_gdown[1], ple_w_gup[1], [])
    return h.reshape(B, S, D)
```

```python
import functools
import math

import jax
import jax.numpy as jnp
from jax import lax
from jax.experimental import pallas as pl
from jax.experimental.pallas import tpu as pltpu

F32 = jnp.float32
BF16 = jnp.bfloat16

EPS = 1e-6
HEAD_DIM = 128
ROPE_THETA = 10000.0
A_GROUPS = ((128, 1), (512, 4), (2048, 16))
N_EXPERTS = 8
TOP_K = 2

LANES = 128
SUBLANES = 8
ATTN_BLOCK = 128
ATTN_HEAD_GROUP = 4
SB_BLOCK = 256
SB_UNROLL = 4
DOWN_CHUNK = 512
VMEM_LIMIT = 56 * 1024 * 1024

_NT = (((1,), (1,)), ((), ()))


def _params(*semantics, vmem_limit=VMEM_LIMIT):
    return pltpu.CompilerParams(dimension_semantics=semantics, vmem_limit_bytes=vmem_limit)


def _sigmoid(x):
    return 1.0 / (1.0 + jnp.exp(-x))


def _residue_perm(n_rows, dil, inverse):
    n = n_rows // dil
    assert dil & (dil - 1) == 0 and n & (n - 1) == 0
    out_row = lax.broadcasted_iota(jnp.int32, (n_rows, n_rows), 0)
    in_row = lax.broadcasted_iota(jnp.int32, (n_rows, n_rows), 1)
    if inverse:
        src = (out_row & (dil - 1)) * n + (out_row >> (dil.bit_length() - 1))
    else:
        src = (out_row & (n - 1)) * dil + (out_row >> (n.bit_length() - 1))
    return in_row == src


def _norm_kernel(*refs, has_delta, n_out):
    refs = list(refs)
    h_ref = refs.pop(0)
    d_ref = refs.pop(0) if has_delta else None
    g_refs = [refs.pop(0) for _ in range(n_out)]
    hsum_ref = refs.pop(0) if has_delta else None
    o_refs = refs
    h = h_ref[...]
    if has_delta:
        h = h + d_ref[...]
        hsum_ref[...] = h
    y = h * lax.rsqrt(jnp.mean(h * h, axis=-1, keepdims=True) + EPS)
    for g_ref, o_ref in zip(g_refs, o_refs):
        o_ref[...] = (y * g_ref[...]).astype(o_ref.dtype)


def _rmsnorm(h, delta, gains, out_dtypes, tm=256):
    T, D = h.shape
    has_delta = delta is not None
    row = pl.BlockSpec((tm, D), lambda i: (i, 0))
    gain = pl.BlockSpec((1, D), lambda i: (0, 0))
    ins = [h] + ([delta] if has_delta else []) + [g.reshape(1, D) for g in gains]
    in_specs = [row] * (2 if has_delta else 1) + [gain] * len(gains)
    out_shape = ([jax.ShapeDtypeStruct((T, D), F32)] if has_delta else []) + [
        jax.ShapeDtypeStruct((T, D), dt) for dt in out_dtypes]
    outs = pl.pallas_call(
        functools.partial(_norm_kernel, has_delta=has_delta, n_out=len(gains)),
        grid=(T // tm,),
        in_specs=in_specs,
        out_specs=[row] * len(out_shape),
        out_shape=out_shape,
        compiler_params=_params("parallel"),
        name="rmsnorm",
    )(*ins)
    return list(outs)


def _norm_dilated_kernel(h_ref, g_ref, *o_refs, dils):
    h = h_ref[...]
    tm = h.shape[0]
    y = (h * lax.rsqrt(jnp.mean(h * h, axis=-1, keepdims=True) + EPS) * g_ref[...]).astype(BF16)
    for o_ref, dil in zip(o_refs, dils):
        if dil == 1:
            o_ref[0, 0] = y
        else:
            perm = _residue_perm(tm, dil, inverse=False).astype(BF16)
            yp = jnp.dot(perm, y, preferred_element_type=F32).astype(BF16)
            n = tm // dil
            for r in range(dil):
                o_ref[0, r] = yp[r * n:(r + 1) * n]


def _rmsnorm_dilated(h, gain, B, S, dils, tm=256):
    T, D = h.shape
    bpb = S // tm
    return pl.pallas_call(
        functools.partial(_norm_dilated_kernel, dils=tuple(dils)),
        grid=(T // tm,),
        in_specs=[pl.BlockSpec((tm, D), lambda i: (i, 0)), pl.BlockSpec((1, D), lambda i: (0, 0))],
        out_specs=[pl.BlockSpec((1, d, tm // d, D), lambda i: (i // bpb, 0, i % bpb, 0)) for d in dils],
        out_shape=[jax.ShapeDtypeStruct((B, d, S // d, D), BF16) for d in dils],
        compiler_params=_params("parallel"),
        name="rmsnorm_dilated",
    )(h, gain.reshape(1, D))


def _mm_kernel(x_ref, w_ref, *rest, has_res):
    acc = jnp.dot(x_ref[...], w_ref[...].astype(BF16), preferred_element_type=F32)
    if has_res:
        r_ref, o_ref = rest
        acc = r_ref[...] + acc
    else:
        (o_ref,) = rest
    o_ref[...] = acc.astype(o_ref.dtype)


def _matmul(x, w, out_dtype, residual=None, tm=1024, tn=512):
    M, K = x.shape
    N = w.shape[1]
    has_res = residual is not None
    in_specs = [pl.BlockSpec((tm, K), lambda i, j: (i, 0)),
                pl.BlockSpec((K, tn), lambda i, j: (0, j))]
    ins = [x, w]
    if has_res:
        in_specs.append(pl.BlockSpec((tm, tn), lambda i, j: (i, j)))
        ins.append(residual)
    return pl.pallas_call(
        functools.partial(_mm_kernel, has_res=has_res),
        grid=(M // tm, N // tn),
        in_specs=in_specs,
        out_specs=pl.BlockSpec((tm, tn), lambda i, j: (i, j)),
        out_shape=jax.ShapeDtypeStruct((M, N), out_dtype),
        compiler_params=_params("parallel", "arbitrary"),
        name="matmul",
    )(*ins)


def _mm_heads_kernel(x_ref, w_ref, o_ref):
    acc = jnp.dot(x_ref[...], w_ref[...].astype(BF16), preferred_element_type=F32)
    for hh in range(o_ref.shape[0]):
        o_ref[hh] = acc[:, hh * HEAD_DIM:(hh + 1) * HEAD_DIM].astype(o_ref.dtype)


def _matmul_heads(x, w, B, S, tm=1024, tn=512):
    M, K = x.shape
    N = w.shape[1]
    bpb = S // tm
    hpt = tn // HEAD_DIM
    return pl.pallas_call(
        _mm_heads_kernel,
        grid=(M // tm, N // tn),
        in_specs=[pl.BlockSpec((tm, K), lambda i, j: (i, 0)),
                  pl.BlockSpec((K, tn), lambda i, j: (0, j))],
        out_specs=pl.BlockSpec((None, hpt, tm, HEAD_DIM), lambda i, j: (i // bpb, j, i % bpb, 0)),
        out_shape=jax.ShapeDtypeStruct((B, N // HEAD_DIM, S, HEAD_DIM), BF16),
        compiler_params=_params("parallel", "arbitrary"),
        name="matmul_heads",
    )(x, w)


def _qkv_kernel(x_ref, w_ref, cos_ref, sin_ref, gq_ref, gk_ref, o_ref, *, q_tiles, qk_tiles):
    j = pl.program_id(1)
    acc = jnp.dot(x_ref[...], w_ref[...].astype(BF16), preferred_element_type=F32)

    @pl.when(j < qk_tiles)
    def _():
        gain = jnp.where(j < q_tiles, gq_ref[...], gk_ref[...])
        cos = cos_ref[...]
        sin = sin_ref[...]
        for hh in range(acc.shape[1] // HEAD_DIM):
            sl = slice(hh * HEAD_DIM, (hh + 1) * HEAD_DIM)
            blk = acc[:, sl]
            y = blk * lax.rsqrt(jnp.mean(blk * blk, axis=-1, keepdims=True) + EPS) * gain
            o_ref[:, sl] = (y * cos + pltpu.roll(y, HEAD_DIM // 2, 1) * sin).astype(o_ref.dtype)

    @pl.when(j >= qk_tiles)
    def _():
        o_ref[...] = acc.astype(o_ref.dtype)


def _rope_tables(S, dil):
    half = HEAD_DIM // 2
    inv = ROPE_THETA ** (-jnp.arange(half, dtype=F32) / half)
    pos = jnp.arange(S, dtype=F32).reshape(S // dil, dil).T.reshape(S)
    ang = pos[:, None] * inv[None, :]
    cos, sin = jnp.cos(ang), jnp.sin(ang)
    return jnp.concatenate([cos, cos], axis=-1), jnp.concatenate([-sin, sin], axis=-1)


def _qkv_proj(x, w, g_q, g_k, S, dil, group, n_groups, tm=1024, tn=512):
    M, K = x.shape
    hw = w.shape[1] // (3 * n_groups)
    tpg = hw // tn
    cos, sin = _rope_tables(S, dil)
    pos_blocks = S // tm
    w_col = lambda i, j: (0, ((j // tpg) * n_groups + group) * tpg + j % tpg)
    return pl.pallas_call(
        functools.partial(_qkv_kernel, q_tiles=tpg, qk_tiles=2 * tpg),
        grid=(M // tm, 3 * tpg),
        in_specs=[pl.BlockSpec((tm, K), lambda i, j: (i, 0)),
                  pl.BlockSpec((K, tn), w_col),
                  pl.BlockSpec((tm, HEAD_DIM), lambda i, j: (i % pos_blocks, 0)),
                  pl.BlockSpec((tm, HEAD_DIM), lambda i, j: (i % pos_blocks, 0)),
                  pl.BlockSpec((1, HEAD_DIM), lambda i, j: (0, 0)),
                  pl.BlockSpec((1, HEAD_DIM), lambda i, j: (0, 0))],
        out_specs=pl.BlockSpec((tm, tn), lambda i, j: (i, j)),
        out_shape=jax.ShapeDtypeStruct((M, 3 * hw), BF16),
        compiler_params=_params("parallel", "arbitrary"),
        name=f"qkv_proj_d{dil}",
    )(x, w, cos, sin, g_q.reshape(1, HEAD_DIM), g_k.reshape(1, HEAD_DIM))


def _dil_attn_kernel(q_ref, kp_ref, kc_ref, vp_ref, vc_ref, o_ref, st_ref, *, n_heads, scale):
    n = pl.program_id(2)
    blk = q_ref.shape[0]
    i = lax.broadcasted_iota(jnp.int32, (blk, blk), 0)
    j = lax.broadcasted_iota(jnp.int32, (blk, blk), 1)
    mask_p = (j >= i) & (n > 0)
    mask_c = j <= i
    lane = lax.broadcasted_iota(jnp.int32, (blk, LANES), 1)
    stats = jnp.zeros((blk, LANES), F32)
    for h0 in range(0, n_heads, ATTN_HEAD_GROUP):
        sls = [slice(h * HEAD_DIM, (h + 1) * HEAD_DIM) for h in range(h0, h0 + ATTN_HEAD_GROUP)]
        s_ps = [lax.dot_general(q_ref[:, sl], kp_ref[:, sl], _NT, preferred_element_type=F32) for sl in sls]
        s_cs = [lax.dot_general(q_ref[:, sl], kc_ref[:, sl], _NT, preferred_element_type=F32) for sl in sls]
        p_ps, p_cs, ls = [], [], []
        for u, (s_p, s_c) in enumerate(zip(s_ps, s_cs)):
            s_p = jnp.where(mask_p, s_p * scale, -jnp.inf)
            s_c = jnp.where(mask_c, s_c * scale, -jnp.inf)
            m = jnp.maximum(jnp.max(s_p, axis=-1, keepdims=True), jnp.max(s_c, axis=-1, keepdims=True))
            p_p = jnp.exp(s_p - m)
            p_c = jnp.exp(s_c - m)
            l = jnp.sum(p_p, axis=-1, keepdims=True) + jnp.sum(p_c, axis=-1, keepdims=True)
            p_ps.append(p_p.astype(BF16))
            p_cs.append(p_c.astype(BF16))
            ls.append(l)
            stats = jnp.where(lane == h0 + u, m + jnp.log(l), stats)
        for sl, p_p, p_c, l in zip(sls, p_ps, p_cs, ls):
            o = (jnp.dot(p_p, vp_ref[:, sl], preferred_element_type=F32)
                 + jnp.dot(p_c, vc_ref[:, sl], preferred_element_type=F32))
            o_ref[:, sl] = (o / l).astype(o_ref.dtype)
    st_ref[...] = stats


def _dilated_attention_group(qkv_g, B, S, dil, n_heads):
    hw = n_heads * HEAD_DIM
    L = S // dil
    nb = L // ATTN_BLOCK
    a = qkv_g.reshape(B, dil, L, 3 * hw)
    blk = (None, None, ATTN_BLOCK, hw)
    cur = lambda c: (lambda b, r, n: (b, r, n, c))
    prev = lambda c: (lambda b, r, n: (b, r, jnp.maximum(n - 1, 0), c))
    return pl.pallas_call(
        functools.partial(_dil_attn_kernel, n_heads=n_heads, scale=1.0 / math.sqrt(HEAD_DIM)),
        grid=(B, dil, nb),
        in_specs=[pl.BlockSpec(blk, cur(0)),
                  pl.BlockSpec(blk, prev(1)), pl.BlockSpec(blk, cur(1)),
                  pl.BlockSpec(blk, prev(2)), pl.BlockSpec(blk, cur(2))],
        out_specs=[pl.BlockSpec(blk, lambda b, r, n: (b, r, n, 0)),
                   pl.BlockSpec((None, None, ATTN_BLOCK, LANES), lambda b, r, n: (b, r, n, 0))],
        out_shape=[jax.ShapeDtypeStruct((B, dil, L, hw), BF16),
                   jax.ShapeDtypeStruct((B, dil, L, LANES), F32)],
        compiler_params=_params("parallel", "parallel", "arbitrary"),
        name=f"dilated_attn_d{dil}",
    )(a, a, a, a, a)


def _merge_kernel(*refs, dils, n_heads):
    G = len(dils)
    o_refs, s_refs, out_ref = refs[:G], refs[G:2 * G], refs[2 * G]
    tm, hw = out_ref.shape
    o, s = [], []
    for o_ref, s_ref, dil in zip(o_refs, s_refs, dils):
        o_g = o_ref[...].reshape(tm, hw)
        s_g = s_ref[...].reshape(tm, LANES)
        if dil == 1:
            o_g = o_g.astype(F32)
        else:
            pinv = _residue_perm(tm, dil, inverse=True)
            o_g = jnp.dot(pinv.astype(BF16), o_g, preferred_element_type=F32)
            s_g = jnp.dot(pinv.astype(F32), s_g, preferred_element_type=F32,
                          precision=lax.Precision.HIGHEST)
        o.append(o_g)
        s.append(s_g)
    m = functools.reduce(jnp.maximum, s)
    e = [jnp.exp(x - m) for x in s]
    den = functools.reduce(lambda a, b: a + b, e)
    w = [x / den for x in e]
    for h in range(n_heads):
        sl = slice(h * HEAD_DIM, (h + 1) * HEAD_DIM)
        acc = w[0][:, h:h + 1] * o[0][:, sl]
        for g in range(1, G):
            acc = acc + w[g][:, h:h + 1] * o[g][:, sl]
        out_ref[:, sl] = acc.astype(out_ref.dtype)


def _merge_groups(os, sts, dils, B, S, n_heads, tm=256):
    hw = n_heads * HEAD_DIM
    bpb = S // tm
    spec = lambda d, w: pl.BlockSpec((None, d, tm // d, w), lambda i: (i // bpb, 0, i % bpb, 0))
    return pl.pallas_call(
        functools.partial(_merge_kernel, dils=tuple(dils), n_heads=n_heads),
        grid=(B * bpb,),
        in_specs=[spec(d, hw) for d in dils] + [spec(d, LANES) for d in dils],
        out_specs=pl.BlockSpec((tm, hw), lambda i: (i, 0)),
        out_shape=jax.ShapeDtypeStruct((B * S, hw), BF16),
        compiler_params=_params("parallel"),
        name="merge_groups",
    )(*os, *sts)


def _sb_kernel(q_ref, k_ref, v_ref, o_ref, *, scale):
    qi = pl.program_id(2)
    tq = q_ref.shape[0]
    q = q_ref[...]
    row = lax.broadcasted_iota(jnp.int32, (tq, tq), 0)
    col = lax.broadcasted_iota(jnp.int32, (tq, tq), 1)
    strict = col < row
    tri = (row > col).astype(BF16)

    def tiles(first, n, carry, diag_first):
        acc, run = carry
        starts = [pl.multiple_of((first - u) * tq, tq) for u in range(n)]
        zs = [lax.dot_general(q, k_ref[pl.ds(s, tq), :], _NT, preferred_element_type=F32) * scale
              for s in starts]
        lks, lss = [], []
        for u, z in enumerate(zs):
            lk = -(jnp.maximum(z, 0.0) + jnp.log(1.0 + jnp.exp(-jnp.abs(z))))
            lss.append(z + lk)
            if diag_first and u == 0:
                lk = jnp.where(strict, lk, 0.0)
            lks.append(lk)
        sufs = [jnp.dot(lk.astype(BF16), tri, preferred_element_type=F32) for lk in lks]
        ps = []
        for u in range(n):
            a = jnp.exp(lss[u] + (sufs[u] + run))
            if diag_first and u == 0:
                a = jnp.where(strict, a, 0.0)
            ps.append(a.astype(BF16))
            run = run + jnp.sum(lks[u], axis=1, keepdims=True)
        for u in range(n):
            acc = acc + jnp.dot(ps[u], v_ref[pl.ds(starts[u], tq), :], preferred_element_type=F32)
        return acc, run

    carry = (jnp.zeros((tq, HEAD_DIM), F32), jnp.zeros((tq, 1), F32))
    rem = qi % SB_UNROLL
    carry = lax.switch(rem, [functools.partial(tiles, qi, 1 + r, diag_first=True)
                             for r in range(SB_UNROLL)], carry)
    carry = lax.fori_loop(
        0, qi // SB_UNROLL,
        lambda t, c: tiles(qi - 1 - rem - t * SB_UNROLL, SB_UNROLL, c, False), carry)
    o_ref[...] = carry[0].astype(o_ref.dtype)


def _stick_breaking(q, kv, n_heads):
    B, _, S, _ = q.shape
    tq = SB_BLOCK
    out = pl.pallas_call(
        functools.partial(_sb_kernel, scale=1.0 / math.sqrt(HEAD_DIM)),
        grid=(B, n_heads, S // tq),
        in_specs=[pl.BlockSpec((None, None, tq, HEAD_DIM), lambda b, h, i: (b, h, i, 0)),
                  pl.BlockSpec((None, None, S, HEAD_DIM), lambda b, h, i: (b, h, 0, 0)),
                  pl.BlockSpec((None, None, S, HEAD_DIM), lambda b, h, i: (b, n_heads + h, 0, 0))],
        out_specs=pl.BlockSpec((None, tq, HEAD_DIM), lambda b, h, i: (b, i, h)),
        out_shape=jax.ShapeDtypeStruct((B, S, n_heads * HEAD_DIM), BF16),
        compiler_params=_params("parallel", "parallel", "arbitrary"),
        name="stick_breaking",
    )(q, kv, kv)
    return out.reshape(B * S, n_heads * HEAD_DIM)


def _swiglu_kernel(te_ref, tv_ref, x_ref, wg_ref, wu_ref, wd_ref, sc_ref, o_ref, *, use_scale):
    t = pl.program_id(0)
    f = pl.program_id(1)
    valid = tv_ref[t] != 0

    @pl.when(valid)
    def _():
        x = x_ref[...]
        g = jnp.dot(x, wg_ref[...].astype(BF16), preferred_element_type=F32)
        u = jnp.dot(x, wu_ref[...].astype(BF16), preferred_element_type=F32)
        mid = (g * _sigmoid(g) * u).astype(BF16)

        @pl.when(f == 0)
        def _():
            o_ref[...] = jnp.zeros_like(o_ref)

        for c in range(0, o_ref.shape[1], DOWN_CHUNK):
            sl = slice(c, c + DOWN_CHUNK)
            o_ref[:, sl] += jnp.dot(mid, wd_ref[:, sl].astype(BF16), preferred_element_type=F32)

        if use_scale:
            @pl.when(f == pl.num_programs(1) - 1)
            def _():
                o_ref[...] *= sc_ref[...]

    @pl.when(jnp.logical_not(valid) & (f == 0))
    def _():
        o_ref[...] = jnp.zeros_like(o_ref)


def _swiglu(x, w_gate, w_up, w_down, tile_expert, tile_rows, row_scale, tm=1024, tf=256):
    P, D = x.shape
    E, _, F = w_gate.shape
    nf = F // tf
    use_scale = row_scale is not None
    if not use_scale:
        row_scale = jnp.ones((P, 1), F32)
    fidx = lambda t, f, tv: jnp.where(tv[t] != 0, f, nf - 1)
    grid_spec = pltpu.PrefetchScalarGridSpec(
        num_scalar_prefetch=2,
        grid=(P // tm, nf),
        in_specs=[pl.BlockSpec((tm, D), lambda t, f, te, tv: (t, 0), pipeline_mode=pl.Buffered(1)),
                  pl.BlockSpec((None, D, tf), lambda t, f, te, tv: (te[t], 0, fidx(t, f, tv))),
                  pl.BlockSpec((None, D, tf), lambda t, f, te, tv: (te[t], 0, fidx(t, f, tv))),
                  pl.BlockSpec((None, tf, D), lambda t, f, te, tv: (te[t], fidx(t, f, tv), 0)),
                  pl.BlockSpec((tm, 1), lambda t, f, te, tv: (t, 0))],
        out_specs=pl.BlockSpec((tm, D), lambda t, f, te, tv: (t, 0), pipeline_mode=pl.Buffered(1)),
    )
    return pl.pallas_call(
        functools.partial(_swiglu_kernel, use_scale=use_scale),
        grid_spec=grid_spec,
        out_shape=jax.ShapeDtypeStruct((P, D), F32),
        compiler_params=_params("parallel", "arbitrary"),
        name="swiglu",
    )(tile_expert, tile_rows, x, w_gate, w_up, w_down, row_scale)


def _router_kernel(h_ref, g_ref, w_ref, xc_ref, idx_ref, p_ref, *, n_experts):
    h = h_ref[...]
    tm, D = h.shape
    x = h * lax.rsqrt(jnp.mean(h * h, axis=-1, keepdims=True) + EPS) * g_ref[...]
    C = D // LANES
    pitch = _chunk_pitch(D)
    for c in range(C):
        xc_ref[pl.ds(c, tm, stride=pitch), :] = x[:, c * LANES:(c + 1) * LANES]
    for c in range(C, pitch):
        xc_ref[pl.ds(c, tm, stride=pitch), :] = jnp.zeros((tm, LANES), F32)
    logits = jnp.dot(x, w_ref[...], preferred_element_type=F32, precision=lax.Precision.HIGHEST)
    lane = lax.broadcasted_iota(jnp.int32, logits.shape, 1)
    logits = jnp.where(lane < n_experts, logits, -jnp.inf)
    m1 = jnp.max(logits, axis=-1, keepdims=True)
    i1 = jnp.min(jnp.where(logits == m1, lane, LANES), axis=-1, keepdims=True)
    rest = jnp.where(lane == i1, -jnp.inf, logits)
    m2 = jnp.max(rest, axis=-1, keepdims=True)
    i2 = jnp.min(jnp.where(rest == m2, lane, LANES), axis=-1, keepdims=True)
    e = jnp.exp(m2 - m1)
    p1 = 1.0 / (1.0 + e)
    p2 = e / (1.0 + e)
    idx_ref[...] = jnp.where(lane == 0, i1, jnp.where(lane == 1, i2, 0))
    p_ref[...] = jnp.where(lane == 0, p1, jnp.where(lane == 1, p2, 0.0))


def _chunk_pitch(D):
    return D // LANES + SUBLANES


def _norm_router(h, gain, w_router, tm=256):
    T, D = h.shape
    E = w_router.shape[1]
    C = _chunk_pitch(D)
    w = jnp.pad(w_router, ((0, 0), (0, LANES - E)))
    xc, idx, prob = pl.pallas_call(
        functools.partial(_router_kernel, n_experts=E),
        grid=(T // tm,),
        in_specs=[pl.BlockSpec((tm, D), lambda i: (i, 0)),
                  pl.BlockSpec((1, D), lambda i: (0, 0)),
                  pl.BlockSpec((D, LANES), lambda i: (0, 0))],
        out_specs=[pl.BlockSpec((tm * C, LANES), lambda i: (i, 0)),
                   pl.BlockSpec((tm, LANES), lambda i: (i, 0)),
                   pl.BlockSpec((tm, LANES), lambda i: (i, 0))],
        out_shape=[jax.ShapeDtypeStruct((T * C, LANES), F32),
                   jax.ShapeDtypeStruct((T, LANES), jnp.int32),
                   jax.ShapeDtypeStruct((T, LANES), F32)],
        compiler_params=_params("parallel"),
        name="norm_router",
    )(h, gain.reshape(1, D), w)
    return xc, idx[:, :TOP_K], prob[:, :TOP_K]


def _route_tables(top_i, top_p, tm, n_tiles):
    T = top_i.shape[0]
    A = T * TOP_K
    flat_e = top_i.reshape(A)
    flat_p = top_p.reshape(A)
    order = jnp.argsort(flat_e, stable=True).astype(jnp.int32)
    counts = jnp.sum(flat_e[:, None] == jnp.arange(N_EXPERTS, dtype=jnp.int32)[None, :], axis=0,
                     dtype=jnp.int32)
    tiles_per = (counts + tm - 1) // tm
    tile_end = jnp.cumsum(tiles_per)
    row_start = (tile_end - tiles_per) * tm
    sorted_start = jnp.cumsum(counts) - counts
    e_sorted = flat_e[order]
    pos_sorted = row_start[e_sorted] + jnp.arange(A, dtype=jnp.int32) - sorted_start[e_sorted]
    P = n_tiles * tm
    row_token = jnp.zeros((P,), jnp.int32).at[pos_sorted].set(order // TOP_K)
    row_scale = jnp.zeros((P,), F32).at[pos_sorted].set(flat_p[order])
    pos = jnp.zeros((A,), jnp.int32).at[order].set(pos_sorted).reshape(T, TOP_K)
    tile_ids = jnp.arange(n_tiles, dtype=jnp.int32)
    used = tile_end[-1]
    te = jnp.sum(tile_ids[:, None] >= tile_end[None, :], axis=1, dtype=jnp.int32)
    te = jnp.minimum(te, N_EXPERTS - 1)
    left = counts[te] - (tile_ids - (tile_end - tiles_per)[te]) * tm
    tile_rows = jnp.where(tile_ids < used, jnp.clip(left, 0, tm), 0).astype(jnp.int32)
    last = jnp.sum((used - 1) >= tile_end, dtype=jnp.int32)
    tile_expert = jnp.where(tile_rows != 0, te, jnp.minimum(last, N_EXPERTS - 1))
    return row_token, row_scale.reshape(P, 1), pos, tile_expert, tile_rows


def _gather_kernel(tok_ref, x_hbm, o_ref, buf, sem):
    tm, D = o_ref.shape
    C = D // LANES
    pitch = _chunk_pitch(D)
    t = pl.program_id(0)

    def row_copy(tile, r):
        slot = tile % 2
        src = pl.multiple_of(tok_ref[tile * tm + r] * pitch, SUBLANES)
        dst = pl.multiple_of(r * pitch, SUBLANES)
        return pltpu.make_async_copy(x_hbm.at[pl.ds(src, C)], buf.at[slot, pl.ds(dst, C)], sem.at[slot])

    def start_tile(tile):
        def body(r, c):
            row_copy(tile, r).start()
            return c
        lax.fori_loop(0, tm, body, 0, unroll=8)

    @pl.when(t == 0)
    def _():
        start_tile(0)

    @pl.when(t + 1 < pl.num_programs(0))
    def _():
        start_tile(t + 1)

    def wait(r, c):
        row_copy(t, r).wait()
        return c

    lax.fori_loop(0, tm, wait, 0, unroll=8)
    slot = t % 2
    for c in range(C):
        o_ref[:, c * LANES:(c + 1) * LANES] = buf[slot, pl.ds(c, tm, stride=pitch), :].astype(o_ref.dtype)


def _gather_rows(xc, row_token, D, tm=256):
    P = row_token.shape[0]
    grid_spec = pltpu.PrefetchScalarGridSpec(
        num_scalar_prefetch=1,
        grid=(P // tm,),
        in_specs=[pl.BlockSpec(memory_space=pl.ANY)],
        out_specs=pl.BlockSpec((tm, D), lambda t, tok: (t, 0)),
        scratch_shapes=[pltpu.VMEM((2, tm * _chunk_pitch(D), LANES), F32), pltpu.SemaphoreType.DMA((2,))],
    )
    return pl.pallas_call(
        _gather_kernel,
        grid_spec=grid_spec,
        out_shape=jax.ShapeDtypeStruct((P, D), BF16),
        compiler_params=_params("arbitrary"),
        name="gather_rows",
    )(row_token, xc)


def _combine_kernel(p0_ref, p1_ref, y_hbm, o_ref, buf0, buf1, sem):
    tm = buf0.shape[0]
    base = pl.program_id(0) * tm

    def copies(r):
        return (pltpu.make_async_copy(y_hbm.at[pl.ds(p0_ref[base + r], 1)], buf0.at[pl.ds(r, 1)], sem),
                pltpu.make_async_copy(y_hbm.at[pl.ds(p1_ref[base + r], 1)], buf1.at[pl.ds(r, 1)], sem))

    def start(r, c):
        for cp in copies(r):
            cp.start()
        return c

    def wait(r, c):
        for cp in copies(r):
            cp.wait()
        return c

    lax.fori_loop(0, tm, start, 0)
    lax.fori_loop(0, tm, wait, 0)
    o_ref[...] = buf0[...] + buf1[...]


def _combine_rows(y, pos, tm=256):
    T = pos.shape[0]
    D = y.shape[1]
    grid_spec = pltpu.PrefetchScalarGridSpec(
        num_scalar_prefetch=2,
        grid=(T // tm,),
        in_specs=[pl.BlockSpec(memory_space=pl.ANY)],
        out_specs=pl.BlockSpec((tm, D), lambda t, p0, p1: (t, 0)),
        scratch_shapes=[pltpu.VMEM((tm, D), F32), pltpu.VMEM((tm, D), F32),
                        pltpu.SemaphoreType.DMA(())],
    )
    return pl.pallas_call(
        _combine_kernel,
        grid_spec=grid_spec,
        out_shape=jax.ShapeDtypeStruct((T, D), F32),
        compiler_params=_params("arbitrary"),
        name="combine_rows",
    )(pos[:, 0], pos[:, 1], y)


def _ple_kernel(h_ref, d_ref, p_ref, g_ref, wgd_ref, wgu_ref, wup_ref, *rest, n_norm):
    gn_refs = rest[:n_norm]
    o_ref = rest[n_norm]
    on_refs = rest[n_norm + 1:]
    h = h_ref[...] + d_ref[...]
    y = h * lax.rsqrt(jnp.mean(h * h, axis=-1, keepdims=True) + EPS) * g_ref[...]
    t = jnp.dot(y.astype(BF16), wgd_ref[...], preferred_element_type=F32)
    gate = _sigmoid(jnp.dot(t.astype(BF16), wgu_ref[...], preferred_element_type=F32))
    up = jnp.dot(p_ref[...].astype(BF16), wup_ref[...], preferred_element_type=F32)
    h = h + gate * up
    o_ref[...] = h
    if n_norm:
        y = h * lax.rsqrt(jnp.mean(h * h, axis=-1, keepdims=True) + EPS)
        for gn_ref, on_ref in zip(gn_refs, on_refs):
            on_ref[...] = (y * gn_ref[...]).astype(on_ref.dtype)


def _ple(h, delta, p_i, g_norm, w_up, w_gdown, w_gup, next_gains, tm=256):
    T, D = h.shape
    R = p_i.shape[1]
    row = pl.BlockSpec((tm, D), lambda i: (i, 0))
    gain = pl.BlockSpec((1, D), lambda i: (0, 0))
    n_norm = len(next_gains)
    outs = pl.pallas_call(
        functools.partial(_ple_kernel, n_norm=n_norm),
        grid=(T // tm,),
        in_specs=[row, row, pl.BlockSpec((tm, R), lambda i: (i, 0)), gain,
                  pl.BlockSpec((D, R), lambda i: (0, 0)),
                  pl.BlockSpec((R, D), lambda i: (0, 0)),
                  pl.BlockSpec((R, D), lambda i: (0, 0))] + [gain] * n_norm,
        out_specs=[row] * (1 + n_norm),
        out_shape=[jax.ShapeDtypeStruct((T, D), F32)] + [jax.ShapeDtypeStruct((T, D), BF16)] * n_norm,
        compiler_params=_params("parallel"),
        name="per_layer_embedding",
    )(h, delta, p_i, g_norm.reshape(1, D), w_gdown.astype(BF16), w_gup.astype(BF16),
      w_up.astype(BF16), *[g.reshape(1, D) for g in next_gains])
    return list(outs)


def kernel(x, p, norm_mix, norm_ffn, norm_ple, a_w_qkv, a_q_norm, a_k_norm, a_w_o, kv_norm, w_kv, b_w_q, b_w_o, ffn_w_gate, ffn_w_up, ffn_w_down, moe_w_router, moe_w_gate, moe_w_up, moe_w_down, ple_w_up, ple_w_gdown, ple_w_gup):
    B, S, D = x.shape
    T = B * S
    assert p.shape[0] == 2 and a_w_qkv.shape[0] == 1 and b_w_q.shape[0] == 1
    G = len(A_GROUPS)
    a_heads = a_w_o.shape[1] // HEAD_DIM
    b_heads = b_w_q.shape[2] // HEAD_DIM
    x2 = x.reshape(T, D)
    p2 = p.reshape(2, T, p.shape[-1])

    dils = [dil for _, dil in A_GROUPS]
    hns = _rmsnorm_dilated(x2, norm_mix[0], B, S, dils)
    os, sts = [], []
    for g, (span, dil) in enumerate(A_GROUPS):
        assert span // dil == ATTN_BLOCK
        qkv_g = _qkv_proj(hns[g].reshape(T, D), a_w_qkv[0], a_q_norm[0], a_k_norm[0], S, dil, g, G)
        o_g, st_g = _dilated_attention_group(qkv_g, B, S, dil, a_heads)
        os.append(o_g)
        sts.append(st_g)
    attn = _merge_groups(os, sts, dils, B, S, a_heads)
    h = _matmul(attn, a_w_o[0], F32, residual=x2, tn=1024)

    (hn,) = _rmsnorm(h, None, [norm_ffn[0]], [BF16])
    tm = 1024
    delta = _swiglu(hn, ffn_w_gate, ffn_w_up, ffn_w_down, jnp.zeros((T // tm,), jnp.int32),
                    jnp.full((T // tm,), tm, jnp.int32), None, tm=tm)
    h, hn_kv, hn_q = _ple(h, delta, p2[0], norm_ple[0], ple_w_up[0], ple_w_gdown[0], ple_w_gup[0],
                          [kv_norm, norm_mix[1]])

    kv = _matmul_heads(hn_kv, w_kv, B, S)
    q = _matmul_heads(hn_q, b_w_q[0], B, S)
    sb = _stick_breaking(q, kv, b_heads)
    h = _matmul(sb, b_w_o[0], F32, residual=h)

    xc, top_i, top_p = _norm_router(h, norm_ffn[1], moe_w_router[0])
    n_tiles = (T * TOP_K) // tm + N_EXPERTS
    row_token, row_scale, pos, tile_expert, tile_rows = _route_tables(top_i, top_p, tm, n_tiles)
    xs = _gather_rows(xc, row_token, D)
    ys = _swiglu(xs, moe_w_gate[0], moe_w_up[0], moe_w_down[0], tile_expert, tile_rows, row_scale, tm=tm)
    delta = _combine_rows(ys, pos)
    (h,) = _ple(h, delta, p2[1], norm_ple[1], ple_w_up[1], ple_w_gdown[1], ple_w_gup[1], [])
    return h.reshape(B, S, D)
```

```python
import functools
import math

import jax
import jax.numpy as jnp
from jax import lax
from jax.experimental import pallas as pl
from jax.experimental.pallas import tpu as pltpu

F32 = jnp.float32
BF16 = jnp.bfloat16

EPS = 1e-6
HEAD_DIM = 128
ROPE_THETA = 10000.0
A_GROUPS = ((128, 1), (512, 4), (2048, 16))
N_EXPERTS = 8
TOP_K = 2

LANES = 128
SUBLANES = 8
ATTN_BLOCK = 128
ATTN_HEAD_GROUP = 4
SB_BLOCK = 256
SB_UNROLL = 4
DOWN_CHUNK = 512
VMEM_LIMIT = 56 * 1024 * 1024

_NT = (((1,), (1,)), ((), ()))


def _params(*semantics, vmem_limit=VMEM_LIMIT):
    return pltpu.CompilerParams(dimension_semantics=semantics, vmem_limit_bytes=vmem_limit)


def _sigmoid(x):
    return 1.0 / (1.0 + jnp.exp(-x))


def _residue_perm(n_rows, dil, inverse):
    n = n_rows // dil
    assert dil & (dil - 1) == 0 and n & (n - 1) == 0
    out_row = lax.broadcasted_iota(jnp.int32, (n_rows, n_rows), 0)
    in_row = lax.broadcasted_iota(jnp.int32, (n_rows, n_rows), 1)
    if inverse:
        src = (out_row & (dil - 1)) * n + (out_row >> (dil.bit_length() - 1))
    else:
        src = (out_row & (n - 1)) * dil + (out_row >> (n.bit_length() - 1))
    return in_row == src


def _norm_kernel(*refs, has_delta, n_out):
    refs = list(refs)
    h_ref = refs.pop(0)
    d_ref = refs.pop(0) if has_delta else None
    g_refs = [refs.pop(0) for _ in range(n_out)]
    hsum_ref = refs.pop(0) if has_delta else None
    o_refs = refs
    h = h_ref[...]
    if has_delta:
        h = h + d_ref[...]
        hsum_ref[...] = h
    y = h * lax.rsqrt(jnp.mean(h * h, axis=-1, keepdims=True) + EPS)
    for g_ref, o_ref in zip(g_refs, o_refs):
        o_ref[...] = (y * g_ref[...]).astype(o_ref.dtype)


def _rmsnorm(h, delta, gains, out_dtypes, tm=256):
    T, D = h.shape
    has_delta = delta is not None
    row = pl.BlockSpec((tm, D), lambda i: (i, 0))
    gain = pl.BlockSpec((1, D), lambda i: (0, 0))
    ins = [h] + ([delta] if has_delta else []) + [g.reshape(1, D) for g in gains]
    in_specs = [row] * (2 if has_delta else 1) + [gain] * len(gains)
    out_shape = ([jax.ShapeDtypeStruct((T, D), F32)] if has_delta else []) + [
        jax.ShapeDtypeStruct((T, D), dt) for dt in out_dtypes]
    outs = pl.pallas_call(
        functools.partial(_norm_kernel, has_delta=has_delta, n_out=len(gains)),
        grid=(T // tm,),
        in_specs=in_specs,
        out_specs=[row] * len(out_shape),
        out_shape=out_shape,
        compiler_params=_params("parallel"),
        name="rmsnorm",
    )(*ins)
    return list(outs)


def _norm_dilated_kernel(h_ref, g_ref, *o_refs, dils):
    h = h_ref[...]
    tm = h.shape[0]
    y = (h * lax.rsqrt(jnp.mean(h * h, axis=-1, keepdims=True) + EPS) * g_ref[...]).astype(BF16)
    for o_ref, dil in zip(o_refs, dils):
        if dil == 1:
            o_ref[0, 0] = y
        else:
            perm = _residue_perm(tm, dil, inverse=False).astype(BF16)
            yp = jnp.dot(perm, y, preferred_element_type=F32).astype(BF16)
            n = tm // dil
            for r in range(dil):
                o_ref[0, r] = yp[r * n:(r + 1) * n]


def _rmsnorm_dilated(h, gain, B, S, dils, tm=256):
    T, D = h.shape
    bpb = S // tm
    return pl.pallas_call(
        functools.partial(_norm_dilated_kernel, dils=tuple(dils)),
        grid=(T // tm,),
        in_specs=[pl.BlockSpec((tm, D), lambda i: (i, 0)), pl.BlockSpec((1, D), lambda i: (0, 0))],
        out_specs=[pl.BlockSpec((1, d, tm // d, D), lambda i: (i // bpb, 0, i % bpb, 0)) for d in dils],
        out_shape=[jax.ShapeDtypeStruct((B, d, S // d, D), BF16) for d in dils],
        compiler_params=_params("parallel"),
        name="rmsnorm_dilated",
    )(h, gain.reshape(1, D))


def _mm_kernel(x_ref, w_ref, *rest, has_res):
    acc = jnp.dot(x_ref[...], w_ref[...].astype(BF16), preferred_element_type=F32)
    if has_res:
        r_ref, o_ref = rest
        acc = r_ref[...] + acc
    else:
        (o_ref,) = rest
    o_ref[...] = acc.astype(o_ref.dtype)


def _matmul(x, w, out_dtype, residual=None, tm=1024, tn=512):
    M, K = x.shape
    N = w.shape[1]
    has_res = residual is not None
    in_specs = [pl.BlockSpec((tm, K), lambda i, j: (i, 0)),
                pl.BlockSpec((K, tn), lambda i, j: (0, j))]
    ins = [x, w]
    if has_res:
        in_specs.append(pl.BlockSpec((tm, tn), lambda i, j: (i, j)))
        ins.append(residual)
    return pl.pallas_call(
        functools.partial(_mm_kernel, has_res=has_res),
        grid=(M // tm, N // tn),
        in_specs=in_specs,
        out_specs=pl.BlockSpec((tm, tn), lambda i, j: (i, j)),
        out_shape=jax.ShapeDtypeStruct((M, N), out_dtype),
        compiler_params=_params("parallel", "arbitrary"),
        name="matmul",
    )(*ins)


def _mm_heads_kernel(x_ref, w_ref, o_ref):
    acc = jnp.dot(x_ref[...], w_ref[...].astype(BF16), preferred_element_type=F32)
    for hh in range(o_ref.shape[0]):
        o_ref[hh] = acc[:, hh * HEAD_DIM:(hh + 1) * HEAD_DIM].astype(o_ref.dtype)


def _matmul_heads(x, w, B, S, tm=1024, tn=512):
    M, K = x.shape
    N = w.shape[1]
    bpb = S // tm
    hpt = tn // HEAD_DIM
    return pl.pallas_call(
        _mm_heads_kernel,
        grid=(M // tm, N // tn),
        in_specs=[pl.BlockSpec((tm, K), lambda i, j: (i, 0)),
                  pl.BlockSpec((K, tn), lambda i, j: (0, j))],
        out_specs=pl.BlockSpec((None, hpt, tm, HEAD_DIM), lambda i, j: (i // bpb, j, i % bpb, 0)),
        out_shape=jax.ShapeDtypeStruct((B, N // HEAD_DIM, S, HEAD_DIM), BF16),
        compiler_params=_params("parallel", "arbitrary"),
        name="matmul_heads",
    )(x, w)


def _qkv_kernel(x_ref, w_ref, cos_ref, sin_ref, gq_ref, gk_ref, o_ref, acc_a, acc_b, *,
                q_tiles, qk_tiles, n_tiles):
    j = pl.program_id(1)

    def matmul(acc_ref):
        acc_ref[...] = jnp.dot(x_ref[...], w_ref[...].astype(BF16), preferred_element_type=F32)

    def finish(acc_ref):
        jp = j - 1
        is_qk = jp < qk_tiles
        gain = jnp.where(jp < q_tiles, gq_ref[...], gk_ref[...])
        cos = cos_ref[...]
        sin = sin_ref[...]
        for hh in range(acc_ref.shape[1] // HEAD_DIM):
            sl = slice(hh * HEAD_DIM, (hh + 1) * HEAD_DIM)
            blk = acc_ref[:, sl]
            y = blk * lax.rsqrt(jnp.mean(blk * blk, axis=-1, keepdims=True) + EPS) * gain
            y = y * cos + pltpu.roll(y, HEAD_DIM // 2, 1) * sin
            o_ref[:, sl] = jnp.where(is_qk, y, blk).astype(o_ref.dtype)

    even = j % 2 == 0

    @pl.when(j == 0)
    def _():
        matmul(acc_a)

    @pl.when((j > 0) & (j < n_tiles) & even)
    def _():
        matmul(acc_a)
        finish(acc_b)

    @pl.when((j < n_tiles) & jnp.logical_not(even))
    def _():
        matmul(acc_b)
        finish(acc_a)

    @pl.when(j == n_tiles)
    def _():
        finish(acc_b if n_tiles % 2 == 0 else acc_a)


def _rope_tables(S, dil):
    half = HEAD_DIM // 2
    inv = ROPE_THETA ** (-jnp.arange(half, dtype=F32) / half)
    pos = jnp.arange(S, dtype=F32).reshape(S // dil, dil).T.reshape(S)
    ang = pos[:, None] * inv[None, :]
    cos, sin = jnp.cos(ang), jnp.sin(ang)
    return jnp.concatenate([cos, cos], axis=-1), jnp.concatenate([-sin, sin], axis=-1)


def _qkv_proj(x, w, g_q, g_k, S, dil, group, n_groups, tm=1024, tn=512):
    M, K = x.shape
    hw = w.shape[1] // (3 * n_groups)
    tpg = hw // tn
    cos, sin = _rope_tables(S, dil)
    pos_blocks = S // tm
    n_tiles = 3 * tpg

    def w_col(i, j):
        jc = jnp.minimum(j, n_tiles - 1)
        return (0, ((jc // tpg) * n_groups + group) * tpg + jc % tpg)

    return pl.pallas_call(
        functools.partial(_qkv_kernel, q_tiles=tpg, qk_tiles=2 * tpg, n_tiles=n_tiles),
        grid=(M // tm, n_tiles + 1),
        in_specs=[pl.BlockSpec((tm, K), lambda i, j: (i, 0)),
                  pl.BlockSpec((K, tn), w_col),
                  pl.BlockSpec((tm, HEAD_DIM), lambda i, j: (i % pos_blocks, 0)),
                  pl.BlockSpec((tm, HEAD_DIM), lambda i, j: (i % pos_blocks, 0)),
                  pl.BlockSpec((1, HEAD_DIM), lambda i, j: (0, 0)),
                  pl.BlockSpec((1, HEAD_DIM), lambda i, j: (0, 0))],
        out_specs=pl.BlockSpec((tm, tn), lambda i, j: (i, jnp.maximum(j - 1, 0))),
        out_shape=jax.ShapeDtypeStruct((M, 3 * hw), BF16),
        scratch_shapes=[pltpu.VMEM((tm, tn), F32), pltpu.VMEM((tm, tn), F32)],
        compiler_params=_params("parallel", "arbitrary"),
        name=f"qkv_proj_d{dil}",
    )(x, w, cos, sin, g_q.reshape(1, HEAD_DIM), g_k.reshape(1, HEAD_DIM))


def _dil_attn_kernel(q_ref, kp_ref, kc_ref, vp_ref, vc_ref, o_ref, st_ref, *, n_heads, scale):
    n = pl.program_id(2)
    blk = q_ref.shape[0]
    i = lax.broadcasted_iota(jnp.int32, (blk, blk), 0)
    j = lax.broadcasted_iota(jnp.int32, (blk, blk), 1)
    mask_p = (j >= i) & (n > 0)
    mask_c = j <= i
    lane = lax.broadcasted_iota(jnp.int32, (blk, LANES), 1)
    stats = jnp.zeros((blk, LANES), F32)
    for h0 in range(0, n_heads, ATTN_HEAD_GROUP):
        sls = [slice(h * HEAD_DIM, (h + 1) * HEAD_DIM) for h in range(h0, h0 + ATTN_HEAD_GROUP)]
        s_ps = [lax.dot_general(q_ref[:, sl], kp_ref[:, sl], _NT, preferred_element_type=F32) for sl in sls]
        s_cs = [lax.dot_general(q_ref[:, sl], kc_ref[:, sl], _NT, preferred_element_type=F32) for sl in sls]
        p_ps, p_cs, ls = [], [], []
        for u, (s_p, s_c) in enumerate(zip(s_ps, s_cs)):
            s_p = jnp.where(mask_p, s_p * scale, -jnp.inf)
            s_c = jnp.where(mask_c, s_c * scale, -jnp.inf)
            m = jnp.maximum(jnp.max(s_p, axis=-1, keepdims=True), jnp.max(s_c, axis=-1, keepdims=True))
            p_p = jnp.exp(s_p - m)
            p_c = jnp.exp(s_c - m)
            l = jnp.sum(p_p, axis=-1, keepdims=True) + jnp.sum(p_c, axis=-1, keepdims=True)
            p_ps.append(p_p.astype(BF16))
            p_cs.append(p_c.astype(BF16))
            ls.append(l)
            stats = jnp.where(lane == h0 + u, m + jnp.log(l), stats)
        for sl, p_p, p_c, l in zip(sls, p_ps, p_cs, ls):
            o = (jnp.dot(p_p, vp_ref[:, sl], preferred_element_type=F32)
                 + jnp.dot(p_c, vc_ref[:, sl], preferred_element_type=F32))
            o_ref[:, sl] = (o / l).astype(o_ref.dtype)
    st_ref[...] = stats


def _dilated_attention_group(qkv_g, B, S, dil, n_heads):
    hw = n_heads * HEAD_DIM
    L = S // dil
    nb = L // ATTN_BLOCK
    a = qkv_g.reshape(B, dil, L, 3 * hw)
    blk = (None, None, ATTN_BLOCK, hw)
    cur = lambda c: (lambda b, r, n: (b, r, n, c))
    prev = lambda c: (lambda b, r, n: (b, r, jnp.maximum(n - 1, 0), c))
    return pl.pallas_call(
        functools.partial(_dil_attn_kernel, n_heads=n_heads, scale=1.0 / math.sqrt(HEAD_DIM)),
        grid=(B, dil, nb),
        in_specs=[pl.BlockSpec(blk, cur(0)),
                  pl.BlockSpec(blk, prev(1)), pl.BlockSpec(blk, cur(1)),
                  pl.BlockSpec(blk, prev(2)), pl.BlockSpec(blk, cur(2))],
        out_specs=[pl.BlockSpec(blk, lambda b, r, n: (b, r, n, 0)),
                   pl.BlockSpec((None, None, ATTN_BLOCK, LANES), lambda b, r, n: (b, r, n, 0))],
        out_shape=[jax.ShapeDtypeStruct((B, dil, L, hw), BF16),
                   jax.ShapeDtypeStruct((B, dil, L, LANES), F32)],
        compiler_params=_params("parallel", "parallel", "arbitrary"),
        name=f"dilated_attn_d{dil}",
    )(a, a, a, a, a)


def _merge_kernel(*refs, dils, n_heads):
    G = len(dils)
    o_refs, s_refs, out_ref = refs[:G], refs[G:2 * G], refs[2 * G]
    tm, hw = out_ref.shape
    o, s = [], []
    for o_ref, s_ref, dil in zip(o_refs, s_refs, dils):
        o_g = o_ref[...].reshape(tm, hw)
        s_g = s_ref[...].reshape(tm, LANES)
        if dil == 1:
            o_g = o_g.astype(F32)
        else:
            pinv = _residue_perm(tm, dil, inverse=True)
            o_g = jnp.dot(pinv.astype(BF16), o_g, preferred_element_type=F32)
            s_g = jnp.dot(pinv.astype(F32), s_g, preferred_element_type=F32,
                          precision=lax.Precision.HIGHEST)
        o.append(o_g)
        s.append(s_g)
    m = functools.reduce(jnp.maximum, s)
    e = [jnp.exp(x - m) for x in s]
    den = functools.reduce(lambda a, b: a + b, e)
    w = [x / den for x in e]
    for h in range(n_heads):
        sl = slice(h * HEAD_DIM, (h + 1) * HEAD_DIM)
        acc = w[0][:, h:h + 1] * o[0][:, sl]
        for g in range(1, G):
            acc = acc + w[g][:, h:h + 1] * o[g][:, sl]
        out_ref[:, sl] = acc.astype(out_ref.dtype)


def _merge_groups(os, sts, dils, B, S, n_heads, tm=256):
    hw = n_heads * HEAD_DIM
    bpb = S // tm
    spec = lambda d, w: pl.BlockSpec((None, d, tm // d, w), lambda i: (i // bpb, 0, i % bpb, 0))
    return pl.pallas_call(
        functools.partial(_merge_kernel, dils=tuple(dils), n_heads=n_heads),
        grid=(B * bpb,),
        in_specs=[spec(d, hw) for d in dils] + [spec(d, LANES) for d in dils],
        out_specs=pl.BlockSpec((tm, hw), lambda i: (i, 0)),
        out_shape=jax.ShapeDtypeStruct((B * S, hw), BF16),
        compiler_params=_params("parallel"),
        name="merge_groups",
    )(*os, *sts)


def _sb_kernel(q_ref, k_ref, v_ref, o_ref, *, scale):
    tq = SB_BLOCK
    row = lax.broadcasted_iota(jnp.int32, (tq, tq), 0)
    col = lax.broadcasted_iota(jnp.int32, (tq, tq), 1)
    strict = col < row
    tri = (row > col).astype(BF16)

    def tiles(q, first, n, carry, diag_first):
        acc, run = carry
        starts = [pl.multiple_of((first - u) * tq, tq) for u in range(n)]
        zs = [lax.dot_general(q, k_ref[pl.ds(s, tq), :], _NT, preferred_element_type=F32) * scale
              for s in starts]
        lks, lss = [], []
        for u, z in enumerate(zs):
            lk = -(jnp.maximum(z, 0.0) + jnp.log(1.0 + jnp.exp(-jnp.abs(z))))
            lss.append(z + lk)
            if diag_first and u == 0:
                lk = jnp.where(strict, lk, 0.0)
            lks.append(lk)
        sufs = [jnp.dot(lk.astype(BF16), tri, preferred_element_type=F32) for lk in lks]
        ps = []
        for u in range(n):
            a = jnp.exp(lss[u] + (sufs[u] + run))
            if diag_first and u == 0:
                a = jnp.where(strict, a, 0.0)
            ps.append(a.astype(BF16))
            run = run + jnp.sum(lks[u], axis=1, keepdims=True)
        for u in range(n):
            acc = acc + jnp.dot(ps[u], v_ref[pl.ds(starts[u], tq), :], preferred_element_type=F32)
        return acc, run

    def q_tile(qi, _):
        rows = pl.ds(pl.multiple_of(qi * tq, tq), tq)
        q = q_ref[rows, :]
        carry = (jnp.zeros((tq, HEAD_DIM), F32), jnp.zeros((tq, 1), F32))
        rem = qi % SB_UNROLL
        carry = lax.switch(rem, [functools.partial(tiles, q, qi, 1 + r, diag_first=True)
                                 for r in range(SB_UNROLL)], carry)
        carry = lax.fori_loop(
            0, qi // SB_UNROLL,
            lambda t, c: tiles(q, qi - 1 - rem - t * SB_UNROLL, SB_UNROLL, c, False), carry)
        o_ref[rows, :] = carry[0].astype(o_ref.dtype)
        return 0

    lax.fori_loop(0, q_ref.shape[0] // tq, q_tile, 0)


def _stick_breaking(q, kv, n_heads):
    B, _, S, _ = q.shape
    head = lambda off: pl.BlockSpec((None, None, S, HEAD_DIM), lambda b, h: (b, off + h, 0, 0))
    out = pl.pallas_call(
        functools.partial(_sb_kernel, scale=1.0 / math.sqrt(HEAD_DIM)),
        grid=(B, n_heads),
        in_specs=[head(0), head(0), head(n_heads)],
        out_specs=pl.BlockSpec((None, S, HEAD_DIM), lambda b, h: (b, 0, h)),
        out_shape=jax.ShapeDtypeStruct((B, S, n_heads * HEAD_DIM), BF16),
        compiler_params=_params("parallel", "arbitrary"),
        name="stick_breaking",
    )(q, kv, kv)
    return out.reshape(B * S, n_heads * HEAD_DIM)


def _swiglu_kernel(te_ref, tv_ref, x_ref, wg_ref, wu_ref, wd_ref, sc_ref, o_ref, *, use_scale):
    t = pl.program_id(0)
    f = pl.program_id(1)
    valid = tv_ref[t] != 0

    @pl.when(valid)
    def _():
        x = x_ref[...]
        g = jnp.dot(x, wg_ref[...].astype(BF16), preferred_element_type=F32)
        u = jnp.dot(x, wu_ref[...].astype(BF16), preferred_element_type=F32)
        mid = (g * _sigmoid(g) * u).astype(BF16)

        @pl.when(f == 0)
        def _():
            o_ref[...] = jnp.zeros_like(o_ref)

        for c in range(0, o_ref.shape[1], DOWN_CHUNK):
            sl = slice(c, c + DOWN_CHUNK)
            o_ref[:, sl] += jnp.dot(mid, wd_ref[:, sl].astype(BF16), preferred_element_type=F32)

        if use_scale:
            @pl.when(f == pl.num_programs(1) - 1)
            def _():
                o_ref[...] *= sc_ref[...]

    @pl.when(jnp.logical_not(valid) & (f == 0))
    def _():
        o_ref[...] = jnp.zeros_like(o_ref)


def _swiglu(x, w_gate, w_up, w_down, tile_expert, tile_rows, row_scale, tm=1024, tf=256):
    P, D = x.shape
    E, _, F = w_gate.shape
    nf = F // tf
    use_scale = row_scale is not None
    if not use_scale:
        row_scale = jnp.ones((P, 1), F32)
    fidx = lambda t, f, tv: jnp.where(tv[t] != 0, f, nf - 1)
    grid_spec = pltpu.PrefetchScalarGridSpec(
        num_scalar_prefetch=2,
        grid=(P // tm, nf),
        in_specs=[pl.BlockSpec((tm, D), lambda t, f, te, tv: (t, 0), pipeline_mode=pl.Buffered(1)),
                  pl.BlockSpec((None, D, tf), lambda t, f, te, tv: (te[t], 0, fidx(t, f, tv))),
                  pl.BlockSpec((None, D, tf), lambda t, f, te, tv: (te[t], 0, fidx(t, f, tv))),
                  pl.BlockSpec((None, tf, D), lambda t, f, te, tv: (te[t], fidx(t, f, tv), 0)),
                  pl.BlockSpec((tm, 1), lambda t, f, te, tv: (t, 0))],
        out_specs=pl.BlockSpec((tm, D), lambda t, f, te, tv: (t, 0), pipeline_mode=pl.Buffered(1)),
    )
    return pl.pallas_call(
        functools.partial(_swiglu_kernel, use_scale=use_scale),
        grid_spec=grid_spec,
        out_shape=jax.ShapeDtypeStruct((P, D), F32),
        compiler_params=_params("parallel", "arbitrary"),
        name="swiglu",
    )(tile_expert, tile_rows, x, w_gate, w_up, w_down, row_scale)


def _router_kernel(h_ref, g_ref, w_ref, xc_ref, idx_ref, p_ref, *, n_experts):
    h = h_ref[...]
    tm, D = h.shape
    x = h * lax.rsqrt(jnp.mean(h * h, axis=-1, keepdims=True) + EPS) * g_ref[...]
    C = D // LANES
    pitch = _chunk_pitch(D)
    for c in range(C):
        xc_ref[pl.ds(c, tm, stride=pitch), :] = x[:, c * LANES:(c + 1) * LANES]
    for c in range(C, pitch):
        xc_ref[pl.ds(c, tm, stride=pitch), :] = jnp.zeros((tm, LANES), F32)
    logits = jnp.dot(x, w_ref[...], preferred_element_type=F32, precision=lax.Precision.HIGHEST)
    lane = lax.broadcasted_iota(jnp.int32, logits.shape, 1)
    logits = jnp.where(lane < n_experts, logits, -jnp.inf)
    m1 = jnp.max(logits, axis=-1, keepdims=True)
    i1 = jnp.min(jnp.where(logits == m1, lane, LANES), axis=-1, keepdims=True)
    rest = jnp.where(lane == i1, -jnp.inf, logits)
    m2 = jnp.max(rest, axis=-1, keepdims=True)
    i2 = jnp.min(jnp.where(rest == m2, lane, LANES), axis=-1, keepdims=True)
    e = jnp.exp(m2 - m1)
    p1 = 1.0 / (1.0 + e)
    p2 = e / (1.0 + e)
    idx_ref[...] = jnp.where(lane == 0, i1, jnp.where(lane == 1, i2, 0))
    p_ref[...] = jnp.where(lane == 0, p1, jnp.where(lane == 1, p2, 0.0))


def _chunk_pitch(D):
    return D // LANES + SUBLANES


def _norm_router(h, gain, w_router, tm=256):
    T, D = h.shape
    E = w_router.shape[1]
    C = _chunk_pitch(D)
    w = jnp.pad(w_router, ((0, 0), (0, LANES - E)))
    xc, idx, prob = pl.pallas_call(
        functools.partial(_router_kernel, n_experts=E),
        grid=(T // tm,),
        in_specs=[pl.BlockSpec((tm, D), lambda i: (i, 0)),
                  pl.BlockSpec((1, D), lambda i: (0, 0)),
                  pl.BlockSpec((D, LANES), lambda i: (0, 0))],
        out_specs=[pl.BlockSpec((tm * C, LANES), lambda i: (i, 0)),
                   pl.BlockSpec((tm, LANES), lambda i: (i, 0)),
                   pl.BlockSpec((tm, LANES), lambda i: (i, 0))],
        out_shape=[jax.ShapeDtypeStruct((T * C, LANES), F32),
                   jax.ShapeDtypeStruct((T, LANES), jnp.int32),
                   jax.ShapeDtypeStruct((T, LANES), F32)],
        compiler_params=_params("parallel"),
        name="norm_router",
    )(h, gain.reshape(1, D), w)
    return xc, idx[:, :TOP_K], prob[:, :TOP_K]


def _route_tables(top_i, top_p, tm, n_tiles):
    T = top_i.shape[0]
    A = T * TOP_K
    flat_e = top_i.reshape(A)
    flat_p = top_p.reshape(A)
    order = jnp.argsort(flat_e, stable=True).astype(jnp.int32)
    counts = jnp.sum(flat_e[:, None] == jnp.arange(N_EXPERTS, dtype=jnp.int32)[None, :], axis=0,
                     dtype=jnp.int32)
    tiles_per = (counts + tm - 1) // tm
    tile_end = jnp.cumsum(tiles_per)
    row_start = (tile_end - tiles_per) * tm
    sorted_start = jnp.cumsum(counts) - counts
    e_sorted = flat_e[order]
    pos_sorted = row_start[e_sorted] + jnp.arange(A, dtype=jnp.int32) - sorted_start[e_sorted]
    P = n_tiles * tm
    row_token = jnp.zeros((P,), jnp.int32).at[pos_sorted].set(order // TOP_K)
    row_scale = jnp.zeros((P,), F32).at[pos_sorted].set(flat_p[order])
    pos = jnp.zeros((A,), jnp.int32).at[order].set(pos_sorted).reshape(T, TOP_K)
    tile_ids = jnp.arange(n_tiles, dtype=jnp.int32)
    used = tile_end[-1]
    te = jnp.sum(tile_ids[:, None] >= tile_end[None, :], axis=1, dtype=jnp.int32)
    te = jnp.minimum(te, N_EXPERTS - 1)
    left = counts[te] - (tile_ids - (tile_end - tiles_per)[te]) * tm
    tile_rows = jnp.where(tile_ids < used, jnp.clip(left, 0, tm), 0).astype(jnp.int32)
    last = jnp.sum((used - 1) >= tile_end, dtype=jnp.int32)
    tile_expert = jnp.where(tile_rows != 0, te, jnp.minimum(last, N_EXPERTS - 1))
    return row_token, row_scale.reshape(P, 1), pos, tile_expert, tile_rows


def _gather_kernel(tok_ref, x_hbm, o_ref, buf, sem):
    tm, D = o_ref.shape
    C = D // LANES
    pitch = _chunk_pitch(D)
    t = pl.program_id(0)

    def row_copy(tile, r):
        slot = tile % 2
        src = pl.multiple_of(tok_ref[tile * tm + r] * pitch, SUBLANES)
        dst = pl.multiple_of(r * pitch, SUBLANES)
        return pltpu.make_async_copy(x_hbm.at[pl.ds(src, C)], buf.at[slot, pl.ds(dst, C)], sem.at[slot])

    def start_tile(tile):
        def body(r8, c):
            for j in range(SUBLANES):
                row_copy(tile, r8 * SUBLANES + j).start(priority=j % 2)
            return c
        lax.fori_loop(0, tm // SUBLANES, body, 0)

    @pl.when(t == 0)
    def _():
        start_tile(0)

    @pl.when(t + 1 < pl.num_programs(0))
    def _():
        start_tile(t + 1)

    def wait(r, c):
        row_copy(t, r).wait()
        return c

    lax.fori_loop(0, tm, wait, 0, unroll=8)
    slot = t % 2
    for c in range(C):
        o_ref[:, c * LANES:(c + 1) * LANES] = buf[slot, pl.ds(c, tm, stride=pitch), :].astype(o_ref.dtype)


def _gather_rows(xc, row_token, D, tm=256):
    P = row_token.shape[0]
    grid_spec = pltpu.PrefetchScalarGridSpec(
        num_scalar_prefetch=1,
        grid=(P // tm,),
        in_specs=[pl.BlockSpec(memory_space=pl.ANY)],
        out_specs=pl.BlockSpec((tm, D), lambda t, tok: (t, 0)),
        scratch_shapes=[pltpu.VMEM((2, tm * _chunk_pitch(D), LANES), F32), pltpu.SemaphoreType.DMA((2,))],
    )
    return pl.pallas_call(
        _gather_kernel,
        grid_spec=grid_spec,
        out_shape=jax.ShapeDtypeStruct((P, D), BF16),
        compiler_params=_params("arbitrary"),
        name="gather_rows",
    )(row_token, xc)


def _combine_kernel(p0_ref, p1_ref, y_hbm, o_ref, buf0, buf1, sem):
    tm = buf0.shape[0]
    base = pl.program_id(0) * tm

    def copies(r):
        return (pltpu.make_async_copy(y_hbm.at[pl.ds(p0_ref[base + r], 1)], buf0.at[pl.ds(r, 1)], sem),
                pltpu.make_async_copy(y_hbm.at[pl.ds(p1_ref[base + r], 1)], buf1.at[pl.ds(r, 1)], sem))

    def start(r, c):
        for j, cp in enumerate(copies(r)):
            cp.start(priority=j)
        return c

    def wait(r, c):
        for cp in copies(r):
            cp.wait()
        return c

    lax.fori_loop(0, tm, start, 0)
    lax.fori_loop(0, tm, wait, 0)
    o_ref[...] = buf0[...] + buf1[...]


def _combine_rows(y, pos, tm=256):
    T = pos.shape[0]
    D = y.shape[1]
    grid_spec = pltpu.PrefetchScalarGridSpec(
        num_scalar_prefetch=2,
        grid=(T // tm,),
        in_specs=[pl.BlockSpec(memory_space=pl.ANY)],
        out_specs=pl.BlockSpec((tm, D), lambda t, p0, p1: (t, 0)),
        scratch_shapes=[pltpu.VMEM((tm, D), F32), pltpu.VMEM((tm, D), F32),
                        pltpu.SemaphoreType.DMA(())],
    )
    return pl.pallas_call(
        _combine_kernel,
        grid_spec=grid_spec,
        out_shape=jax.ShapeDtypeStruct((T, D), F32),
        compiler_params=_params("arbitrary"),
        name="combine_rows",
    )(pos[:, 0], pos[:, 1], y)


def _ple_kernel(h_ref, d_ref, p_ref, g_ref, wgd_ref, wgu_ref, wup_ref, *rest, n_norm):
    gn_refs = rest[:n_norm]
    o_ref = rest[n_norm]
    on_refs = rest[n_norm + 1:]
    h = h_ref[...] + d_ref[...]
    y = h * lax.rsqrt(jnp.mean(h * h, axis=-1, keepdims=True) + EPS) * g_ref[...]
    t = jnp.dot(y.astype(BF16), wgd_ref[...], preferred_element_type=F32)
    gate = _sigmoid(jnp.dot(t.astype(BF16), wgu_ref[...], preferred_element_type=F32))
    up = jnp.dot(p_ref[...].astype(BF16), wup_ref[...], preferred_element_type=F32)
    h = h + gate * up
    o_ref[...] = h
    if n_norm:
        y = h * lax.rsqrt(jnp.mean(h * h, axis=-1, keepdims=True) + EPS)
        for gn_ref, on_ref in zip(gn_refs, on_refs):
            on_ref[...] = (y * gn_ref[...]).astype(on_ref.dtype)


def _ple(h, delta, p_i, g_norm, w_up, w_gdown, w_gup, next_gains, tm=256):
    T, D = h.shape
    R = p_i.shape[1]
    row = pl.BlockSpec((tm, D), lambda i: (i, 0))
    gain = pl.BlockSpec((1, D), lambda i: (0, 0))
    n_norm = len(next_gains)
    outs = pl.pallas_call(
        functools.partial(_ple_kernel, n_norm=n_norm),
        grid=(T // tm,),
        in_specs=[row, row, pl.BlockSpec((tm, R), lambda i: (i, 0)), gain,
                  pl.BlockSpec((D, R), lambda i: (0, 0)),
                  pl.BlockSpec((R, D), lambda i: (0, 0)),
                  pl.BlockSpec((R, D), lambda i: (0, 0))] + [gain] * n_norm,
        out_specs=[row] * (1 + n_norm),
        out_shape=[jax.ShapeDtypeStruct((T, D), F32)] + [jax.ShapeDtypeStruct((T, D), BF16)] * n_norm,
        compiler_params=_params("parallel"),
        name="per_layer_embedding",
    )(h, delta, p_i, g_norm.reshape(1, D), w_gdown.astype(BF16), w_gup.astype(BF16),
      w_up.astype(BF16), *[g.reshape(1, D) for g in next_gains])
    return list(outs)


def kernel(x, p, norm_mix, norm_ffn, norm_ple, a_w_qkv, a_q_norm, a_k_norm, a_w_o, kv_norm, w_kv, b_w_q, b_w_o, ffn_w_gate, ffn_w_up, ffn_w_down, moe_w_router, moe_w_gate, moe_w_up, moe_w_down, ple_w_up, ple_w_gdown, ple_w_gup):
    B, S, D = x.shape
    T = B * S
    assert p.shape[0] == 2 and a_w_qkv.shape[0] == 1 and b_w_q.shape[0] == 1
    G = len(A_GROUPS)
    a_heads = a_w_o.shape[1] // HEAD_DIM
    b_heads = b_w_q.shape[2] // HEAD_DIM
    x2 = x.reshape(T, D)
    p2 = p.reshape(2, T, p.shape[-1])

    dils = [dil for _, dil in A_GROUPS]
    hns = _rmsnorm_dilated(x2, norm_mix[0], B, S, dils)
    os, sts = [], []
    for g, (span, dil) in enumerate(A_GROUPS):
        assert span // dil == ATTN_BLOCK
        qkv_g = _qkv_proj(hns[g].reshape(T, D), a_w_qkv[0], a_q_norm[0], a_k_norm[0], S, dil, g, G)
        o_g, st_g = _dilated_attention_group(qkv_g, B, S, dil, a_heads)
        os.append(o_g)
        sts.append(st_g)
    attn = _merge_groups(os, sts, dils, B, S, a_heads)
    h = _matmul(attn, a_w_o[0], F32, residual=x2, tn=1024)

    (hn,) = _rmsnorm(h, None, [norm_ffn[0]], [BF16])
    tm = 1024
    delta = _swiglu(hn, ffn_w_gate, ffn_w_up, ffn_w_down, jnp.zeros((T // tm,), jnp.int32),
                    jnp.full((T // tm,), tm, jnp.int32), None, tm=tm)
    h, hn_kv, hn_q = _ple(h, delta, p2[0], norm_ple[0], ple_w_up[0], ple_w_gdown[0], ple_w_gup[0],
                          [kv_norm, norm_mix[1]])

    kv = _matmul_heads(hn_kv, w_kv, B, S)
    q = _matmul_heads(hn_q, b_w_q[0], B, S)
    sb = _stick_breaking(q, kv, b_heads)
    h = _matmul(sb, b_w_o[0], F32, residual=h)

    xc, top_i, top_p = _norm_router(h, norm_ffn[1], moe_w_router[0])
    n_tiles = (T * TOP_K) // tm + N_EXPERTS
    row_token, row_scale, pos, tile_expert, tile_rows = _route_tables(top_i, top_p, tm, n_tiles)
    xs = _gather_rows(xc, row_token, D)
    ys = _swiglu(xs, moe_w_gate[0], moe_w_up[0], moe_w_down[0], tile_expert, tile_rows, row_scale, tm=tm)
    delta = _combine_rows(ys, pos)
    (h,) = _ple(h, delta, p2[1], norm_ple[1], ple_w_up[1], ple_w_gdown[1], ple_w_gup[1], [])
    return h.reshape(B, S, D)
```

```python
import functools
import math

import jax
import jax.numpy as jnp
from jax import lax
from jax.experimental import pallas as pl
from jax.experimental.pallas import tpu as pltpu

F32 = jnp.float32
BF16 = jnp.bfloat16

EPS = 1e-6
HEAD_DIM = 128
ROPE_THETA = 10000.0
A_GROUPS = ((128, 1), (512, 4), (2048, 16))
N_EXPERTS = 8
TOP_K = 2

LANES = 128
SUBLANES = 8
ATTN_BLOCK = 128
ATTN_HEAD_GROUP = 4
SB_BLOCK = 256
SB_UNROLL = 4
SB_DEAD = -104.0
DOWN_CHUNK = 512
VMEM_LIMIT = 56 * 1024 * 1024

_NT = (((1,), (1,)), ((), ()))


def _params(*semantics, vmem_limit=VMEM_LIMIT):
    return pltpu.CompilerParams(dimension_semantics=semantics, vmem_limit_bytes=vmem_limit)


def _sigmoid(x):
    return 1.0 / (1.0 + jnp.exp(-x))


def _residue_perm(n_rows, dil, inverse):
    n = n_rows // dil
    assert dil & (dil - 1) == 0 and n & (n - 1) == 0
    out_row = lax.broadcasted_iota(jnp.int32, (n_rows, n_rows), 0)
    in_row = lax.broadcasted_iota(jnp.int32, (n_rows, n_rows), 1)
    if inverse:
        src = (out_row & (dil - 1)) * n + (out_row >> (dil.bit_length() - 1))
    else:
        src = (out_row & (n - 1)) * dil + (out_row >> (n.bit_length() - 1))
    return in_row == src


def _norm_kernel(*refs, has_delta, n_out):
    refs = list(refs)
    h_ref = refs.pop(0)
    d_ref = refs.pop(0) if has_delta else None
    g_refs = [refs.pop(0) for _ in range(n_out)]
    hsum_ref = refs.pop(0) if has_delta else None
    o_refs = refs
    h = h_ref[...]
    if has_delta:
        h = h + d_ref[...]
        hsum_ref[...] = h
    y = h * lax.rsqrt(jnp.mean(h * h, axis=-1, keepdims=True) + EPS)
    for g_ref, o_ref in zip(g_refs, o_refs):
        o_ref[...] = (y * g_ref[...]).astype(o_ref.dtype)


def _rmsnorm(h, delta, gains, out_dtypes, tm=256):
    T, D = h.shape
    has_delta = delta is not None
    row = pl.BlockSpec((tm, D), lambda i: (i, 0))
    gain = pl.BlockSpec((1, D), lambda i: (0, 0))
    ins = [h] + ([delta] if has_delta else []) + [g.reshape(1, D) for g in gains]
    in_specs = [row] * (2 if has_delta else 1) + [gain] * len(gains)
    out_shape = ([jax.ShapeDtypeStruct((T, D), F32)] if has_delta else []) + [
        jax.ShapeDtypeStruct((T, D), dt) for dt in out_dtypes]
    outs = pl.pallas_call(
        functools.partial(_norm_kernel, has_delta=has_delta, n_out=len(gains)),
        grid=(T // tm,),
        in_specs=in_specs,
        out_specs=[row] * len(out_shape),
        out_shape=out_shape,
        compiler_params=_params("parallel"),
        name="rmsnorm",
    )(*ins)
    return list(outs)


def _norm_dilated_kernel(h_ref, g_ref, *o_refs, dils):
    h = h_ref[...]
    tm = h.shape[0]
    y = (h * lax.rsqrt(jnp.mean(h * h, axis=-1, keepdims=True) + EPS) * g_ref[...]).astype(BF16)
    for o_ref, dil in zip(o_refs, dils):
        if dil == 1:
            o_ref[0, 0] = y
        else:
            perm = _residue_perm(tm, dil, inverse=False).astype(BF16)
            yp = jnp.dot(perm, y, preferred_element_type=F32).astype(BF16)
            n = tm // dil
            for r in range(dil):
                o_ref[0, r] = yp[r * n:(r + 1) * n]


def _rmsnorm_dilated(h, gain, B, S, dils, tm=256):
    T, D = h.shape
    bpb = S // tm
    return pl.pallas_call(
        functools.partial(_norm_dilated_kernel, dils=tuple(dils)),
        grid=(T // tm,),
        in_specs=[pl.BlockSpec((tm, D), lambda i: (i, 0)), pl.BlockSpec((1, D), lambda i: (0, 0))],
        out_specs=[pl.BlockSpec((1, d, tm // d, D), lambda i: (i // bpb, 0, i % bpb, 0)) for d in dils],
        out_shape=[jax.ShapeDtypeStruct((B, d, S // d, D), BF16) for d in dils],
        compiler_params=_params("parallel"),
        name="rmsnorm_dilated",
    )(h, gain.reshape(1, D))


def _mm_kernel(x_ref, w_ref, *rest, has_res):
    acc = jnp.dot(x_ref[...], w_ref[...].astype(BF16), preferred_element_type=F32)
    if has_res:
        r_ref, o_ref = rest
        acc = r_ref[...] + acc
    else:
        (o_ref,) = rest
    o_ref[...] = acc.astype(o_ref.dtype)


def _matmul(x, w, out_dtype, residual=None, tm=1024, tn=512):
    M, K = x.shape
    N = w.shape[1]
    has_res = residual is not None
    in_specs = [pl.BlockSpec((tm, K), lambda i, j: (i, 0)),
                pl.BlockSpec((K, tn), lambda i, j: (0, j))]
    ins = [x, w]
    if has_res:
        in_specs.append(pl.BlockSpec((tm, tn), lambda i, j: (i, j)))
        ins.append(residual)
    return pl.pallas_call(
        functools.partial(_mm_kernel, has_res=has_res),
        grid=(M // tm, N // tn),
        in_specs=in_specs,
        out_specs=pl.BlockSpec((tm, tn), lambda i, j: (i, j)),
        out_shape=jax.ShapeDtypeStruct((M, N), out_dtype),
        compiler_params=_params("parallel", "arbitrary"),
        name="matmul",
    )(*ins)


def _mm_heads_kernel(x_ref, w_ref, o_ref):
    acc = jnp.dot(x_ref[...], w_ref[...].astype(BF16), preferred_element_type=F32)
    for hh in range(o_ref.shape[0]):
        o_ref[hh] = acc[:, hh * HEAD_DIM:(hh + 1) * HEAD_DIM].astype(o_ref.dtype)


def _matmul_heads(x, w, B, S, tm=1024, tn=512):
    M, K = x.shape
    N = w.shape[1]
    bpb = S // tm
    hpt = tn // HEAD_DIM
    return pl.pallas_call(
        _mm_heads_kernel,
        grid=(M // tm, N // tn),
        in_specs=[pl.BlockSpec((tm, K), lambda i, j: (i, 0)),
                  pl.BlockSpec((K, tn), lambda i, j: (0, j))],
        out_specs=pl.BlockSpec((None, hpt, tm, HEAD_DIM), lambda i, j: (i // bpb, j, i % bpb, 0)),
        out_shape=jax.ShapeDtypeStruct((B, N // HEAD_DIM, S, HEAD_DIM), BF16),
        compiler_params=_params("parallel", "arbitrary"),
        name="matmul_heads",
    )(x, w)


def _qkv_kernel(x_ref, w_ref, cos_ref, sin_ref, gq_ref, gk_ref, o_ref, acc_a, acc_b, *,
                q_tiles, qk_tiles, n_tiles):
    j = pl.program_id(1)

    def matmul(acc_ref):
        acc_ref[...] = jnp.dot(x_ref[...], w_ref[...].astype(BF16), preferred_element_type=F32)

    def finish(acc_ref):
        jp = j - 1
        is_qk = jp < qk_tiles
        gain = jnp.where(jp < q_tiles, gq_ref[...], gk_ref[...])
        cos = cos_ref[...]
        sin = sin_ref[...]
        for hh in range(acc_ref.shape[1] // HEAD_DIM):
            sl = slice(hh * HEAD_DIM, (hh + 1) * HEAD_DIM)
            blk = acc_ref[:, sl]
            y = blk * lax.rsqrt(jnp.mean(blk * blk, axis=-1, keepdims=True) + EPS) * gain
            y = y * cos + pltpu.roll(y, HEAD_DIM // 2, 1) * sin
            o_ref[:, sl] = jnp.where(is_qk, y, blk).astype(o_ref.dtype)

    even = j % 2 == 0

    @pl.when(j == 0)
    def _():
        matmul(acc_a)

    @pl.when((j > 0) & (j < n_tiles) & even)
    def _():
        matmul(acc_a)
        finish(acc_b)

    @pl.when((j < n_tiles) & jnp.logical_not(even))
    def _():
        matmul(acc_b)
        finish(acc_a)

    @pl.when(j == n_tiles)
    def _():
        finish(acc_b if n_tiles % 2 == 0 else acc_a)


def _rope_tables(S, dil):
    half = HEAD_DIM // 2
    inv = ROPE_THETA ** (-jnp.arange(half, dtype=F32) / half)
    pos = jnp.arange(S, dtype=F32).reshape(S // dil, dil).T.reshape(S)
    ang = pos[:, None] * inv[None, :]
    cos, sin = jnp.cos(ang), jnp.sin(ang)
    return jnp.concatenate([cos, cos], axis=-1), jnp.concatenate([-sin, sin], axis=-1)


def _qkv_proj(x, w, g_q, g_k, S, dil, group, n_groups, tm=1024, tn=512):
    M, K = x.shape
    hw = w.shape[1] // (3 * n_groups)
    tpg = hw // tn
    cos, sin = _rope_tables(S, dil)
    pos_blocks = S // tm
    n_tiles = 3 * tpg

    def w_col(i, j):
        jc = jnp.minimum(j, n_tiles - 1)
        return (0, ((jc // tpg) * n_groups + group) * tpg + jc % tpg)

    return pl.pallas_call(
        functools.partial(_qkv_kernel, q_tiles=tpg, qk_tiles=2 * tpg, n_tiles=n_tiles),
        grid=(M // tm, n_tiles + 1),
        in_specs=[pl.BlockSpec((tm, K), lambda i, j: (i, 0)),
                  pl.BlockSpec((K, tn), w_col),
                  pl.BlockSpec((tm, HEAD_DIM), lambda i, j: (i % pos_blocks, 0)),
                  pl.BlockSpec((tm, HEAD_DIM), lambda i, j: (i % pos_blocks, 0)),
                  pl.BlockSpec((1, HEAD_DIM), lambda i, j: (0, 0)),
                  pl.BlockSpec((1, HEAD_DIM), lambda i, j: (0, 0))],
        out_specs=pl.BlockSpec((tm, tn), lambda i, j: (i, jnp.maximum(j - 1, 0))),
        out_shape=jax.ShapeDtypeStruct((M, 3 * hw), BF16),
        scratch_shapes=[pltpu.VMEM((tm, tn), F32), pltpu.VMEM((tm, tn), F32)],
        compiler_params=_params("parallel", "arbitrary"),
        name=f"qkv_proj_d{dil}",
    )(x, w, cos, sin, g_q.reshape(1, HEAD_DIM), g_k.reshape(1, HEAD_DIM))


def _dil_attn_kernel(q_ref, kp_ref, kc_ref, vp_ref, vc_ref, o_ref, st_ref, *, n_heads, scale):
    n = pl.program_id(2)
    blk = q_ref.shape[0]
    i = lax.broadcasted_iota(jnp.int32, (blk, blk), 0)
    j = lax.broadcasted_iota(jnp.int32, (blk, blk), 1)
    mask_p = (j >= i) & (n > 0)
    mask_c = j <= i
    lane = lax.broadcasted_iota(jnp.int32, (blk, LANES), 1)
    stats = jnp.zeros((blk, LANES), F32)
    for h0 in range(0, n_heads, ATTN_HEAD_GROUP):
        sls = [slice(h * HEAD_DIM, (h + 1) * HEAD_DIM) for h in range(h0, h0 + ATTN_HEAD_GROUP)]
        s_ps = [lax.dot_general(q_ref[:, sl], kp_ref[:, sl], _NT, preferred_element_type=F32) for sl in sls]
        s_cs = [lax.dot_general(q_ref[:, sl], kc_ref[:, sl], _NT, preferred_element_type=F32) for sl in sls]
        p_ps, p_cs, ls = [], [], []
        for u, (s_p, s_c) in enumerate(zip(s_ps, s_cs)):
            s_p = jnp.where(mask_p, s_p * scale, -jnp.inf)
            s_c = jnp.where(mask_c, s_c * scale, -jnp.inf)
            m = jnp.maximum(jnp.max(s_p, axis=-1, keepdims=True), jnp.max(s_c, axis=-1, keepdims=True))
            p_p = jnp.exp(s_p - m)
            p_c = jnp.exp(s_c - m)
            l = jnp.sum(p_p, axis=-1, keepdims=True) + jnp.sum(p_c, axis=-1, keepdims=True)
            p_ps.append(p_p.astype(BF16))
            p_cs.append(p_c.astype(BF16))
            ls.append(l)
            stats = jnp.where(lane == h0 + u, m + jnp.log(l), stats)
        for sl, p_p, p_c, l in zip(sls, p_ps, p_cs, ls):
            o = (jnp.dot(p_p, vp_ref[:, sl], preferred_element_type=F32)
                 + jnp.dot(p_c, vc_ref[:, sl], preferred_element_type=F32))
            o_ref[:, sl] = (o / l).astype(o_ref.dtype)
    st_ref[...] = stats


def _dilated_attention_group(qkv_g, B, S, dil, n_heads):
    hw = n_heads * HEAD_DIM
    L = S // dil
    nb = L // ATTN_BLOCK
    a = qkv_g.reshape(B, dil, L, 3 * hw)
    blk = (None, None, ATTN_BLOCK, hw)
    cur = lambda c: (lambda b, r, n: (b, r, n, c))
    prev = lambda c: (lambda b, r, n: (b, r, jnp.maximum(n - 1, 0), c))
    return pl.pallas_call(
        functools.partial(_dil_attn_kernel, n_heads=n_heads, scale=1.0 / math.sqrt(HEAD_DIM)),
        grid=(B, dil, nb),
        in_specs=[pl.BlockSpec(blk, cur(0)),
                  pl.BlockSpec(blk, prev(1)), pl.BlockSpec(blk, cur(1)),
                  pl.BlockSpec(blk, prev(2)), pl.BlockSpec(blk, cur(2))],
        out_specs=[pl.BlockSpec(blk, lambda b, r, n: (b, r, n, 0)),
                   pl.BlockSpec((None, None, ATTN_BLOCK, LANES), lambda b, r, n: (b, r, n, 0))],
        out_shape=[jax.ShapeDtypeStruct((B, dil, L, hw), BF16),
                   jax.ShapeDtypeStruct((B, dil, L, LANES), F32)],
        compiler_params=_params("parallel", "parallel", "arbitrary"),
        name=f"dilated_attn_d{dil}",
    )(a, a, a, a, a)


def _merge_kernel(*refs, dils, n_heads):
    G = len(dils)
    o_refs, s_refs, out_ref = refs[:G], refs[G:2 * G], refs[2 * G]
    tm, hw = out_ref.shape
    o, s = [], []
    for o_ref, s_ref, dil in zip(o_refs, s_refs, dils):
        o_g = o_ref[...].reshape(tm, hw)
        s_g = s_ref[...].reshape(tm, LANES)
        if dil == 1:
            o_g = o_g.astype(F32)
        else:
            pinv = _residue_perm(tm, dil, inverse=True)
            o_g = jnp.dot(pinv.astype(BF16), o_g, preferred_element_type=F32)
            s_g = jnp.dot(pinv.astype(F32), s_g, preferred_element_type=F32,
                          precision=lax.Precision.HIGHEST)
        o.append(o_g)
        s.append(s_g)
    m = functools.reduce(jnp.maximum, s)
    e = [jnp.exp(x - m) for x in s]
    den = functools.reduce(lambda a, b: a + b, e)
    w = [x / den for x in e]
    for h in range(n_heads):
        sl = slice(h * HEAD_DIM, (h + 1) * HEAD_DIM)
        acc = w[0][:, h:h + 1] * o[0][:, sl]
        for g in range(1, G):
            acc = acc + w[g][:, h:h + 1] * o[g][:, sl]
        out_ref[:, sl] = acc.astype(out_ref.dtype)


def _merge_groups(os, sts, dils, B, S, n_heads, tm=256):
    hw = n_heads * HEAD_DIM
    bpb = S // tm
    spec = lambda d, w: pl.BlockSpec((None, d, tm // d, w), lambda i: (i // bpb, 0, i % bpb, 0))
    return pl.pallas_call(
        functools.partial(_merge_kernel, dils=tuple(dils), n_heads=n_heads),
        grid=(B * bpb,),
        in_specs=[spec(d, hw) for d in dils] + [spec(d, LANES) for d in dils],
        out_specs=pl.BlockSpec((tm, hw), lambda i: (i, 0)),
        out_shape=jax.ShapeDtypeStruct((B * S, hw), BF16),
        compiler_params=_params("parallel"),
        name="merge_groups",
    )(*os, *sts)


def _sb_kernel(q_ref, k_ref, v_ref, o_ref, *, scale):
    tq = SB_BLOCK
    row = lax.broadcasted_iota(jnp.int32, (tq, tq), 0)
    col = lax.broadcasted_iota(jnp.int32, (tq, tq), 1)
    strict = col < row
    tri = (row > col).astype(BF16)

    def tiles(q, first, n, carry, diag_first):
        acc, run = carry
        starts = [pl.multiple_of((first - u) * tq, tq) for u in range(n)]
        zs = [lax.dot_general(q, k_ref[pl.ds(s, tq), :], _NT, preferred_element_type=F32) * scale
              for s in starts]
        lks, lss = [], []
        for u, z in enumerate(zs):
            lk = -(jnp.maximum(z, 0.0) + jnp.log(1.0 + jnp.exp(-jnp.abs(z))))
            lss.append(z + lk)
            if diag_first and u == 0:
                lk = jnp.where(strict, lk, 0.0)
            lks.append(lk)
        sufs = [jnp.dot(lk.astype(BF16), tri, preferred_element_type=F32) for lk in lks]
        ps = []
        for u in range(n):
            a = jnp.exp(lss[u] + (sufs[u] + run))
            if diag_first and u == 0:
                a = jnp.where(strict, a, 0.0)
            ps.append(a.astype(BF16))
            run = run + jnp.sum(lks[u], axis=1, keepdims=True)
        for u in range(n):
            acc = acc + jnp.dot(ps[u], v_ref[pl.ds(starts[u], tq), :], preferred_element_type=F32)
        return acc, run

    def q_tile(qi, _):
        rows = pl.ds(pl.multiple_of(qi * tq, tq), tq)
        q = q_ref[rows, :]
        carry = (jnp.zeros((tq, HEAD_DIM), F32), jnp.zeros((tq, 1), F32))
        acc, run = lax.cond(qi == 0, functools.partial(tiles, q, qi, 1, diag_first=True),
                            functools.partial(tiles, q, qi, 2, diag_first=True), carry)
        alive = lambda r: (jnp.max(r) > SB_DEAD).astype(jnp.int32)
        rest = jnp.maximum(qi - 1, 0)
        rem = rest % SB_UNROLL

        def walk(n, first, count, state):
            def body(c):
                t, acc, run, _ = c
                acc, run = tiles(q, first - t * n, n, (acc, run), False)
                return t + 1, acc, run, alive(run)
            return lax.while_loop(lambda c: (c[0] < count) & (c[3] != 0), body, (0,) + state)[1:]

        state = (acc, run, alive(run))
        state = walk(1, rest - 1, rem, state)
        state = walk(SB_UNROLL, rest - 1 - rem, rest // SB_UNROLL, state)
        o_ref[rows, :] = state[0].astype(o_ref.dtype)
        return 0

    lax.fori_loop(0, q_ref.shape[0] // tq, q_tile, 0)


def _stick_breaking(q, kv, n_heads):
    B, _, S, _ = q.shape
    head = lambda off: pl.BlockSpec((None, None, S, HEAD_DIM), lambda b, h: (b, off + h, 0, 0))
    out = pl.pallas_call(
        functools.partial(_sb_kernel, scale=1.0 / math.sqrt(HEAD_DIM)),
        grid=(B, n_heads),
        in_specs=[head(0), head(0), head(n_heads)],
        out_specs=pl.BlockSpec((None, S, HEAD_DIM), lambda b, h: (b, 0, h)),
        out_shape=jax.ShapeDtypeStruct((B, S, n_heads * HEAD_DIM), BF16),
        compiler_params=_params("parallel", "arbitrary"),
        name="stick_breaking",
    )(q, kv, kv)
    return out.reshape(B * S, n_heads * HEAD_DIM)


def _swiglu_kernel(te_ref, tv_ref, x_ref, wg_ref, wu_ref, wd_ref, sc_ref, o_ref, *, use_scale):
    t = pl.program_id(0)
    f = pl.program_id(1)
    valid = tv_ref[t] != 0

    @pl.when(valid)
    def _():
        x = x_ref[...]
        g = jnp.dot(x, wg_ref[...].astype(BF16), preferred_element_type=F32)
        u = jnp.dot(x, wu_ref[...].astype(BF16), preferred_element_type=F32)
        mid = (g * _sigmoid(g) * u).astype(BF16)

        @pl.when(f == 0)
        def _():
            o_ref[...] = jnp.zeros_like(o_ref)

        for c in range(0, o_ref.shape[1], DOWN_CHUNK):
            sl = slice(c, c + DOWN_CHUNK)
            o_ref[:, sl] += jnp.dot(mid, wd_ref[:, sl].astype(BF16), preferred_element_type=F32)

        if use_scale:
            @pl.when(f == pl.num_programs(1) - 1)
            def _():
                o_ref[...] *= sc_ref[...]

    @pl.when(jnp.logical_not(valid) & (f == 0))
    def _():
        o_ref[...] = jnp.zeros_like(o_ref)


def _swiglu(x, w_gate, w_up, w_down, tile_expert, tile_rows, row_scale, tm=1024, tf=256):
    P, D = x.shape
    E, _, F = w_gate.shape
    nf = F // tf
    use_scale = row_scale is not None
    if not use_scale:
        row_scale = jnp.ones((P, 1), F32)
    fidx = lambda t, f, tv: jnp.where(tv[t] != 0, f, nf - 1)
    grid_spec = pltpu.PrefetchScalarGridSpec(
        num_scalar_prefetch=2,
        grid=(P // tm, nf),
        in_specs=[pl.BlockSpec((tm, D), lambda t, f, te, tv: (t, 0), pipeline_mode=pl.Buffered(1)),
                  pl.BlockSpec((None, D, tf), lambda t, f, te, tv: (te[t], 0, fidx(t, f, tv))),
                  pl.BlockSpec((None, D, tf), lambda t, f, te, tv: (te[t], 0, fidx(t, f, tv))),
                  pl.BlockSpec((None, tf, D), lambda t, f, te, tv: (te[t], fidx(t, f, tv), 0)),
                  pl.BlockSpec((tm, 1), lambda t, f, te, tv: (t, 0))],
        out_specs=pl.BlockSpec((tm, D), lambda t, f, te, tv: (t, 0), pipeline_mode=pl.Buffered(1)),
    )
    return pl.pallas_call(
        functools.partial(_swiglu_kernel, use_scale=use_scale),
        grid_spec=grid_spec,
        out_shape=jax.ShapeDtypeStruct((P, D), F32),
        compiler_params=_params("parallel", "arbitrary"),
        name="swiglu",
    )(tile_expert, tile_rows, x, w_gate, w_up, w_down, row_scale)


def _router_kernel(h_ref, g_ref, w_ref, xc_ref, idx_ref, p_ref, *, n_experts):
    h = h_ref[...]
    tm, D = h.shape
    x = h * lax.rsqrt(jnp.mean(h * h, axis=-1, keepdims=True) + EPS) * g_ref[...]
    C = D // LANES
    pitch = _chunk_pitch(D)
    for c in range(C):
        xc_ref[pl.ds(c, tm, stride=pitch), :] = x[:, c * LANES:(c + 1) * LANES]
    for c in range(C, pitch):
        xc_ref[pl.ds(c, tm, stride=pitch), :] = jnp.zeros((tm, LANES), F32)
    logits = jnp.dot(x, w_ref[...], preferred_element_type=F32, precision=lax.Precision.HIGHEST)
    lane = lax.broadcasted_iota(jnp.int32, logits.shape, 1)
    logits = jnp.where(lane < n_experts, logits, -jnp.inf)
    m1 = jnp.max(logits, axis=-1, keepdims=True)
    i1 = jnp.min(jnp.where(logits == m1, lane, LANES), axis=-1, keepdims=True)
    rest = jnp.where(lane == i1, -jnp.inf, logits)
    m2 = jnp.max(rest, axis=-1, keepdims=True)
    i2 = jnp.min(jnp.where(rest == m2, lane, LANES), axis=-1, keepdims=True)
    e = jnp.exp(m2 - m1)
    p1 = 1.0 / (1.0 + e)
    p2 = e / (1.0 + e)
    idx_ref[...] = jnp.where(lane == 0, i1, jnp.where(lane == 1, i2, 0))
    p_ref[...] = jnp.where(lane == 0, p1, jnp.where(lane == 1, p2, 0.0))


def _chunk_pitch(D):
    return D // LANES + SUBLANES


def _norm_router(h, gain, w_router, tm=256):
    T, D = h.shape
    E = w_router.shape[1]
    C = _chunk_pitch(D)
    w = jnp.pad(w_router, ((0, 0), (0, LANES - E)))
    xc, idx, prob = pl.pallas_call(
        functools.partial(_router_kernel, n_experts=E),
        grid=(T // tm,),
        in_specs=[pl.BlockSpec((tm, D), lambda i: (i, 0)),
                  pl.BlockSpec((1, D), lambda i: (0, 0)),
                  pl.BlockSpec((D, LANES), lambda i: (0, 0))],
        out_specs=[pl.BlockSpec((tm * C, LANES), lambda i: (i, 0)),
                   pl.BlockSpec((tm, LANES), lambda i: (i, 0)),
                   pl.BlockSpec((tm, LANES), lambda i: (i, 0))],
        out_shape=[jax.ShapeDtypeStruct((T * C, LANES), F32),
                   jax.ShapeDtypeStruct((T, LANES), jnp.int32),
                   jax.ShapeDtypeStruct((T, LANES), F32)],
        compiler_params=_params("parallel"),
        name="norm_router",
    )(h, gain.reshape(1, D), w)
    return xc, idx[:, :TOP_K], prob[:, :TOP_K]


def _route_tables(top_i, top_p, tm, n_tiles):
    T = top_i.shape[0]
    A = T * TOP_K
    flat_e = top_i.reshape(A)
    flat_p = top_p.reshape(A)
    order = jnp.argsort(flat_e, stable=True).astype(jnp.int32)
    counts = jnp.sum(flat_e[:, None] == jnp.arange(N_EXPERTS, dtype=jnp.int32)[None, :], axis=0,
                     dtype=jnp.int32)
    tiles_per = (counts + tm - 1) // tm
    tile_end = jnp.cumsum(tiles_per)
    row_start = (tile_end - tiles_per) * tm
    sorted_start = jnp.cumsum(counts) - counts
    e_sorted = flat_e[order]
    pos_sorted = row_start[e_sorted] + jnp.arange(A, dtype=jnp.int32) - sorted_start[e_sorted]
    P = n_tiles * tm
    row_token = jnp.zeros((P,), jnp.int32).at[pos_sorted].set(order // TOP_K)
    row_scale = jnp.zeros((P,), F32).at[pos_sorted].set(flat_p[order])
    pos = jnp.zeros((A,), jnp.int32).at[order].set(pos_sorted).reshape(T, TOP_K)
    tile_ids = jnp.arange(n_tiles, dtype=jnp.int32)
    used = tile_end[-1]
    te = jnp.sum(tile_ids[:, None] >= tile_end[None, :], axis=1, dtype=jnp.int32)
    te = jnp.minimum(te, N_EXPERTS - 1)
    left = counts[te] - (tile_ids - (tile_end - tiles_per)[te]) * tm
    tile_rows = jnp.where(tile_ids < used, jnp.clip(left, 0, tm), 0).astype(jnp.int32)
    last = jnp.sum((used - 1) >= tile_end, dtype=jnp.int32)
    tile_expert = jnp.where(tile_rows != 0, te, jnp.minimum(last, N_EXPERTS - 1))
    return row_token, row_scale.reshape(P, 1), pos, tile_expert, tile_rows


def _gather_kernel(tok_ref, x_hbm, o_ref, buf, sem):
    tm, D = o_ref.shape
    C = D // LANES
    pitch = _chunk_pitch(D)
    t = pl.program_id(0)

    def row_copy(tile, r):
        slot = tile % 2
        src = pl.multiple_of(tok_ref[tile * tm + r] * pitch, SUBLANES)
        dst = pl.multiple_of(r * pitch, SUBLANES)
        return pltpu.make_async_copy(x_hbm.at[pl.ds(src, C)], buf.at[slot, pl.ds(dst, C)], sem.at[slot])

    def start_tile(tile):
        def body(r8, c):
            for j in range(SUBLANES):
                row_copy(tile, r8 * SUBLANES + j).start(priority=j % 2)
            return c
        lax.fori_loop(0, tm // SUBLANES, body, 0)

    @pl.when(t == 0)
    def _():
        start_tile(0)

    @pl.when(t + 1 < pl.num_programs(0))
    def _():
        start_tile(t + 1)

    def wait(r, c):
        row_copy(t, r).wait()
        return c

    lax.fori_loop(0, tm, wait, 0, unroll=8)
    slot = t % 2
    for c in range(C):
        o_ref[:, c * LANES:(c + 1) * LANES] = buf[slot, pl.ds(c, tm, stride=pitch), :].astype(o_ref.dtype)


def _gather_rows(xc, row_token, D, tm=256):
    P = row_token.shape[0]
    grid_spec = pltpu.PrefetchScalarGridSpec(
        num_scalar_prefetch=1,
        grid=(P // tm,),
        in_specs=[pl.BlockSpec(memory_space=pl.ANY)],
        out_specs=pl.BlockSpec((tm, D), lambda t, tok: (t, 0)),
        scratch_shapes=[pltpu.VMEM((2, tm * _chunk_pitch(D), LANES), F32), pltpu.SemaphoreType.DMA((2,))],
    )
    return pl.pallas_call(
        _gather_kernel,
        grid_spec=grid_spec,
        out_shape=jax.ShapeDtypeStruct((P, D), BF16),
        compiler_params=_params("arbitrary"),
        name="gather_rows",
    )(row_token, xc)


def _combine_kernel(p0_ref, p1_ref, y_hbm, o_ref, buf0, buf1, sem):
    tm = buf0.shape[0]
    base = pl.program_id(0) * tm

    def copies(r):
        return (pltpu.make_async_copy(y_hbm.at[pl.ds(p0_ref[base + r], 1)], buf0.at[pl.ds(r, 1)], sem),
                pltpu.make_async_copy(y_hbm.at[pl.ds(p1_ref[base + r], 1)], buf1.at[pl.ds(r, 1)], sem))

    def start(r, c):
        for j, cp in enumerate(copies(r)):
            cp.start(priority=j)
        return c

    def wait(r, c):
        for cp in copies(r):
            cp.wait()
        return c

    lax.fori_loop(0, tm, start, 0)
    lax.fori_loop(0, tm, wait, 0)
    o_ref[...] = buf0[...] + buf1[...]


def _combine_rows(y, pos, tm=256):
    T = pos.shape[0]
    D = y.shape[1]
    grid_spec = pltpu.PrefetchScalarGridSpec(
        num_scalar_prefetch=2,
        grid=(T // tm,),
        in_specs=[pl.BlockSpec(memory_space=pl.ANY)],
        out_specs=pl.BlockSpec((tm, D), lambda t, p0, p1: (t, 0)),
        scratch_shapes=[pltpu.VMEM((tm, D), F32), pltpu.VMEM((tm, D), F32),
                        pltpu.SemaphoreType.DMA(())],
    )
    return pl.pallas_call(
        _combine_kernel,
        grid_spec=grid_spec,
        out_shape=jax.ShapeDtypeStruct((T, D), F32),
        compiler_params=_params("arbitrary"),
        name="combine_rows",
    )(pos[:, 0], pos[:, 1], y)


def _ple_kernel(h_ref, d_ref, p_ref, g_ref, wgd_ref, wgu_ref, wup_ref, *rest, n_norm):
    gn_refs = rest[:n_norm]
    o_ref = rest[n_norm]
    on_refs = rest[n_norm + 1:]
    h = h_ref[...] + d_ref[...]
    y = h * lax.rsqrt(jnp.mean(h * h, axis=-1, keepdims=True) + EPS) * g_ref[...]
    t = jnp.dot(y.astype(BF16), wgd_ref[...], preferred_element_type=F32)
    gate = _sigmoid(jnp.dot(t.astype(BF16), wgu_ref[...], preferred_element_type=F32))
    up = jnp.dot(p_ref[...].astype(BF16), wup_ref[...], preferred_element_type=F32)
    h = h + gate * up
    o_ref[...] = h
    if n_norm:
        y = h * lax.rsqrt(jnp.mean(h * h, axis=-1, keepdims=True) + EPS)
        for gn_ref, on_ref in zip(gn_refs, on_refs):
            on_ref[...] = (y * gn_ref[...]).astype(on_ref.dtype)


def _ple(h, delta, p_i, g_norm, w_up, w_gdown, w_gup, next_gains, tm=256):
    T, D = h.shape
    R = p_i.shape[1]
    row = pl.BlockSpec((tm, D), lambda i: (i, 0))
    gain = pl.BlockSpec((1, D), lambda i: (0, 0))
    n_norm = len(next_gains)
    outs = pl.pallas_call(
        functools.partial(_ple_kernel, n_norm=n_norm),
        grid=(T // tm,),
        in_specs=[row, row, pl.BlockSpec((tm, R), lambda i: (i, 0)), gain,
                  pl.BlockSpec((D, R), lambda i: (0, 0)),
                  pl.BlockSpec((R, D), lambda i: (0, 0)),
                  pl.BlockSpec((R, D), lambda i: (0, 0))] + [gain] * n_norm,
        out_specs=[row] * (1 + n_norm),
        out_shape=[jax.ShapeDtypeStruct((T, D), F32)] + [jax.ShapeDtypeStruct((T, D), BF16)] * n_norm,
        compiler_params=_params("parallel"),
        name="per_layer_embedding",
    )(h, delta, p_i, g_norm.reshape(1, D), w_gdown.astype(BF16), w_gup.astype(BF16),
      w_up.astype(BF16), *[g.reshape(1, D) for g in next_gains])
    return list(outs)


def kernel(x, p, norm_mix, norm_ffn, norm_ple, a_w_qkv, a_q_norm, a_k_norm, a_w_o, kv_norm, w_kv, b_w_q, b_w_o, ffn_w_gate, ffn_w_up, ffn_w_down, moe_w_router, moe_w_gate, moe_w_up, moe_w_down, ple_w_up, ple_w_gdown, ple_w_gup):
    B, S, D = x.shape
    T = B * S
    assert p.shape[0] == 2 and a_w_qkv.shape[0] == 1 and b_w_q.shape[0] == 1
    G = len(A_GROUPS)
    a_heads = a_w_o.shape[1] // HEAD_DIM
    b_heads = b_w_q.shape[2] // HEAD_DIM
    x2 = x.reshape(T, D)
    p2 = p.reshape(2, T, p.shape[-1])

    dils = [dil for _, dil in A_GROUPS]
    hns = _rmsnorm_dilated(x2, norm_mix[0], B, S, dils)
    os, sts = [], []
    for g, (span, dil) in enumerate(A_GROUPS):
        assert span // dil == ATTN_BLOCK
        qkv_g = _qkv_proj(hns[g].reshape(T, D), a_w_qkv[0], a_q_norm[0], a_k_norm[0], S, dil, g, G)
        o_g, st_g = _dilated_attention_group(qkv_g, B, S, dil, a_heads)
        os.append(o_g)
        sts.append(st_g)
    attn = _merge_groups(os, sts, dils, B, S, a_heads)
    h = _matmul(attn, a_w_o[0], F32, residual=x2, tn=1024)

    (hn,) = _rmsnorm(h, None, [norm_ffn[0]], [BF16])
    tm = 1024
    delta = _swiglu(hn, ffn_w_gate, ffn_w_up, ffn_w_down, jnp.zeros((T // tm,), jnp.int32),
                    jnp.full((T // tm,), tm, jnp.int32), None, tm=tm)
    h, hn_kv, hn_q = _ple(h, delta, p2[0], norm_ple[0], ple_w_up[0], ple_w_gdown[0], ple_w_gup[0],
                          [kv_norm, norm_mix[1]])

    kv = _matmul_heads(hn_kv, w_kv, B, S)
    q = _matmul_heads(hn_q, b_w_q[0], B, S)
    sb = _stick_breaking(q, kv, b_heads)
    h = _matmul(sb, b_w_o[0], F32, residual=h)

    xc, top_i, top_p = _norm_router(h, norm_ffn[1], moe_w_router[0])
    n_tiles = (T * TOP_K) // tm + N_EXPERTS
    row_token, row_scale, pos, tile_expert, tile_rows = _route_tables(top_i, top_p, tm, n_tiles)
    xs = _gather_rows(xc, row_token, D)
    ys = _swiglu(xs, moe_w_gate[0], moe_w_up[0], moe_w_down[0], tile_expert, tile_rows, row_scale, tm=tm)
    delta = _combine_rows(ys, pos)
    (h,) = _ple(h, delta, p2[1], norm_ple[1], ple_w_up[1], ple_w_gdown[1], ple_w_gup[1], [])
    return h.reshape(B, S, D)
```

```python
import functools
import math

import jax
import jax.numpy as jnp
from jax import lax
from jax.experimental import pallas as pl
from jax.experimental.pallas import tpu as pltpu

F32 = jnp.float32
BF16 = jnp.bfloat16

EPS = 1e-6
HEAD_DIM = 128
ROPE_THETA = 10000.0
A_GROUPS = ((128, 1), (512, 4), (2048, 16))
N_EXPERTS = 8
TOP_K = 2

LANES = 128
SUBLANES = 8
ATTN_BLOCK = 128
ATTN_HEAD_GROUP = 4
SB_BLOCK = 256
SB_UNROLL = 4
SB_DEAD = -104.0
ROW_STEP = 256
DOWN_CHUNK = 512
VMEM_LIMIT = 56 * 1024 * 1024

_NT = (((1,), (1,)), ((), ()))


def _params(*semantics, vmem_limit=VMEM_LIMIT):
    return pltpu.CompilerParams(dimension_semantics=semantics, vmem_limit_bytes=vmem_limit)


def _sigmoid(x):
    return 1.0 / (1.0 + jnp.exp(-x))


def _residue_perm(n_rows, dil, inverse):
    n = n_rows // dil
    assert dil & (dil - 1) == 0 and n & (n - 1) == 0
    out_row = lax.broadcasted_iota(jnp.int32, (n_rows, n_rows), 0)
    in_row = lax.broadcasted_iota(jnp.int32, (n_rows, n_rows), 1)
    if inverse:
        src = (out_row & (dil - 1)) * n + (out_row >> (dil.bit_length() - 1))
    else:
        src = (out_row & (n - 1)) * dil + (out_row >> (n.bit_length() - 1))
    return in_row == src


def _norm_kernel(*refs, has_delta, n_out):
    refs = list(refs)
    h_ref = refs.pop(0)
    d_ref = refs.pop(0) if has_delta else None
    g_refs = [refs.pop(0) for _ in range(n_out)]
    hsum_ref = refs.pop(0) if has_delta else None
    o_refs = refs
    h = h_ref[...]
    if has_delta:
        h = h + d_ref[...]
        hsum_ref[...] = h
    y = h * lax.rsqrt(jnp.mean(h * h, axis=-1, keepdims=True) + EPS)
    for g_ref, o_ref in zip(g_refs, o_refs):
        o_ref[...] = (y * g_ref[...]).astype(o_ref.dtype)


def _rmsnorm(h, delta, gains, out_dtypes, tm=256):
    T, D = h.shape
    has_delta = delta is not None
    row = pl.BlockSpec((tm, D), lambda i: (i, 0))
    gain = pl.BlockSpec((1, D), lambda i: (0, 0))
    ins = [h] + ([delta] if has_delta else []) + [g.reshape(1, D) for g in gains]
    in_specs = [row] * (2 if has_delta else 1) + [gain] * len(gains)
    out_shape = ([jax.ShapeDtypeStruct((T, D), F32)] if has_delta else []) + [
        jax.ShapeDtypeStruct((T, D), dt) for dt in out_dtypes]
    outs = pl.pallas_call(
        functools.partial(_norm_kernel, has_delta=has_delta, n_out=len(gains)),
        grid=(T // tm,),
        in_specs=in_specs,
        out_specs=[row] * len(out_shape),
        out_shape=out_shape,
        compiler_params=_params("parallel"),
        name="rmsnorm",
    )(*ins)
    return list(outs)


def _norm_dilated_kernel(h_ref, g_ref, *o_refs, dils):
    h = h_ref[...]
    tm = h.shape[0]
    y = (h * lax.rsqrt(jnp.mean(h * h, axis=-1, keepdims=True) + EPS) * g_ref[...]).astype(BF16)
    for o_ref, dil in zip(o_refs, dils):
        if dil == 1:
            o_ref[0, 0] = y
        else:
            perm = _residue_perm(tm, dil, inverse=False).astype(BF16)
            yp = jnp.dot(perm, y, preferred_element_type=F32).astype(BF16)
            n = tm // dil
            for r in range(dil):
                o_ref[0, r] = yp[r * n:(r + 1) * n]


def _rmsnorm_dilated(h, gain, B, S, dils, tm=256):
    T, D = h.shape
    bpb = S // tm
    return pl.pallas_call(
        functools.partial(_norm_dilated_kernel, dils=tuple(dils)),
        grid=(T // tm,),
        in_specs=[pl.BlockSpec((tm, D), lambda i: (i, 0)), pl.BlockSpec((1, D), lambda i: (0, 0))],
        out_specs=[pl.BlockSpec((1, d, tm // d, D), lambda i: (i // bpb, 0, i % bpb, 0)) for d in dils],
        out_shape=[jax.ShapeDtypeStruct((B, d, S // d, D), BF16) for d in dils],
        compiler_params=_params("parallel"),
        name="rmsnorm_dilated",
    )(h, gain.reshape(1, D))


def _mm_kernel(x_ref, w_ref, *rest, has_res):
    acc = jnp.dot(x_ref[...], w_ref[...].astype(BF16), preferred_element_type=F32)
    if has_res:
        r_ref, o_ref = rest
        acc = r_ref[...] + acc
    else:
        (o_ref,) = rest
    o_ref[...] = acc.astype(o_ref.dtype)


def _matmul(x, w, out_dtype, residual=None, tm=1024, tn=512):
    M, K = x.shape
    N = w.shape[1]
    has_res = residual is not None
    in_specs = [pl.BlockSpec((tm, K), lambda i, j: (i, 0)),
                pl.BlockSpec((K, tn), lambda i, j: (0, j))]
    ins = [x, w]
    if has_res:
        in_specs.append(pl.BlockSpec((tm, tn), lambda i, j: (i, j)))
        ins.append(residual)
    return pl.pallas_call(
        functools.partial(_mm_kernel, has_res=has_res),
        grid=(M // tm, N // tn),
        in_specs=in_specs,
        out_specs=pl.BlockSpec((tm, tn), lambda i, j: (i, j)),
        out_shape=jax.ShapeDtypeStruct((M, N), out_dtype),
        compiler_params=_params("parallel", "arbitrary"),
        name="matmul",
    )(*ins)


def _mm_heads_kernel(x_ref, w_ref, o_ref):
    acc = jnp.dot(x_ref[...], w_ref[...].astype(BF16), preferred_element_type=F32)
    for hh in range(o_ref.shape[0]):
        o_ref[hh] = acc[:, hh * HEAD_DIM:(hh + 1) * HEAD_DIM].astype(o_ref.dtype)


def _matmul_heads(x, w, B, S, tm=1024, tn=512):
    M, K = x.shape
    N = w.shape[1]
    bpb = S // tm
    hpt = tn // HEAD_DIM
    return pl.pallas_call(
        _mm_heads_kernel,
        grid=(M // tm, N // tn),
        in_specs=[pl.BlockSpec((tm, K), lambda i, j: (i, 0)),
                  pl.BlockSpec((K, tn), lambda i, j: (0, j))],
        out_specs=pl.BlockSpec((None, hpt, tm, HEAD_DIM), lambda i, j: (i // bpb, j, i % bpb, 0)),
        out_shape=jax.ShapeDtypeStruct((B, N // HEAD_DIM, S, HEAD_DIM), BF16),
        compiler_params=_params("parallel", "arbitrary"),
        name="matmul_heads",
    )(x, w)


def _qkv_kernel(x_ref, w_ref, cos_ref, sin_ref, gq_ref, gk_ref, o_ref, acc_a, acc_b, *,
                q_tiles, qk_tiles, n_tiles):
    j = pl.program_id(1)

    def matmul(acc_ref):
        acc_ref[...] = jnp.dot(x_ref[...], w_ref[...].astype(BF16), preferred_element_type=F32)

    def finish(acc_ref):
        jp = j - 1
        is_qk = jp < qk_tiles
        gain = jnp.where(jp < q_tiles, gq_ref[...], gk_ref[...])
        cos = cos_ref[...]
        sin = sin_ref[...]
        for hh in range(acc_ref.shape[1] // HEAD_DIM):
            sl = slice(hh * HEAD_DIM, (hh + 1) * HEAD_DIM)
            blk = acc_ref[:, sl]
            y = blk * lax.rsqrt(jnp.mean(blk * blk, axis=-1, keepdims=True) + EPS) * gain
            y = y * cos + pltpu.roll(y, HEAD_DIM // 2, 1) * sin
            o_ref[:, sl] = jnp.where(is_qk, y, blk).astype(o_ref.dtype)

    even = j % 2 == 0

    @pl.when(j == 0)
    def _():
        matmul(acc_a)

    @pl.when((j > 0) & (j < n_tiles) & even)
    def _():
        matmul(acc_a)
        finish(acc_b)

    @pl.when((j < n_tiles) & jnp.logical_not(even))
    def _():
        matmul(acc_b)
        finish(acc_a)

    @pl.when(j == n_tiles)
    def _():
        finish(acc_b if n_tiles % 2 == 0 else acc_a)


def _rope_tables(S, dil):
    half = HEAD_DIM // 2
    inv = ROPE_THETA ** (-jnp.arange(half, dtype=F32) / half)
    pos = jnp.arange(S, dtype=F32).reshape(S // dil, dil).T.reshape(S)
    ang = pos[:, None] * inv[None, :]
    cos, sin = jnp.cos(ang), jnp.sin(ang)
    return jnp.concatenate([cos, cos], axis=-1), jnp.concatenate([-sin, sin], axis=-1)


def _qkv_proj(x, w, g_q, g_k, S, dil, group, n_groups, tm=1024, tn=512):
    M, K = x.shape
    hw = w.shape[1] // (3 * n_groups)
    tpg = hw // tn
    cos, sin = _rope_tables(S, dil)
    pos_blocks = S // tm
    n_tiles = 3 * tpg

    def w_col(i, j):
        jc = jnp.minimum(j, n_tiles - 1)
        return (0, ((jc // tpg) * n_groups + group) * tpg + jc % tpg)

    return pl.pallas_call(
        functools.partial(_qkv_kernel, q_tiles=tpg, qk_tiles=2 * tpg, n_tiles=n_tiles),
        grid=(M // tm, n_tiles + 1),
        in_specs=[pl.BlockSpec((tm, K), lambda i, j: (i, 0)),
                  pl.BlockSpec((K, tn), w_col),
                  pl.BlockSpec((tm, HEAD_DIM), lambda i, j: (i % pos_blocks, 0)),
                  pl.BlockSpec((tm, HEAD_DIM), lambda i, j: (i % pos_blocks, 0)),
                  pl.BlockSpec((1, HEAD_DIM), lambda i, j: (0, 0)),
                  pl.BlockSpec((1, HEAD_DIM), lambda i, j: (0, 0))],
        out_specs=pl.BlockSpec((tm, tn), lambda i, j: (i, jnp.maximum(j - 1, 0))),
        out_shape=jax.ShapeDtypeStruct((M, 3 * hw), BF16),
        scratch_shapes=[pltpu.VMEM((tm, tn), F32), pltpu.VMEM((tm, tn), F32)],
        compiler_params=_params("parallel", "arbitrary"),
        name=f"qkv_proj_d{dil}",
    )(x, w, cos, sin, g_q.reshape(1, HEAD_DIM), g_k.reshape(1, HEAD_DIM))


def _dil_attn_kernel(q_ref, kp_ref, kc_ref, vp_ref, vc_ref, o_ref, st_ref, *, n_heads, scale):
    n = pl.program_id(2)
    blk = q_ref.shape[0]
    i = lax.broadcasted_iota(jnp.int32, (blk, blk), 0)
    j = lax.broadcasted_iota(jnp.int32, (blk, blk), 1)
    mask_p = (j >= i) & (n > 0)
    mask_c = j <= i
    lane = lax.broadcasted_iota(jnp.int32, (blk, LANES), 1)
    stats = jnp.zeros((blk, LANES), F32)
    for h0 in range(0, n_heads, ATTN_HEAD_GROUP):
        sls = [slice(h * HEAD_DIM, (h + 1) * HEAD_DIM) for h in range(h0, h0 + ATTN_HEAD_GROUP)]
        s_ps = [lax.dot_general(q_ref[:, sl], kp_ref[:, sl], _NT, preferred_element_type=F32) for sl in sls]
        s_cs = [lax.dot_general(q_ref[:, sl], kc_ref[:, sl], _NT, preferred_element_type=F32) for sl in sls]
        p_ps, p_cs, ls = [], [], []
        for u, (s_p, s_c) in enumerate(zip(s_ps, s_cs)):
            s_p = jnp.where(mask_p, s_p * scale, -jnp.inf)
            s_c = jnp.where(mask_c, s_c * scale, -jnp.inf)
            m = jnp.maximum(jnp.max(s_p, axis=-1, keepdims=True), jnp.max(s_c, axis=-1, keepdims=True))
            p_p = jnp.exp(s_p - m)
            p_c = jnp.exp(s_c - m)
            l = jnp.sum(p_p, axis=-1, keepdims=True) + jnp.sum(p_c, axis=-1, keepdims=True)
            p_ps.append(p_p.astype(BF16))
            p_cs.append(p_c.astype(BF16))
            ls.append(l)
            stats = jnp.where(lane == h0 + u, m + jnp.log(l), stats)
        for sl, p_p, p_c, l in zip(sls, p_ps, p_cs, ls):
            o = (jnp.dot(p_p, vp_ref[:, sl], preferred_element_type=F32)
                 + jnp.dot(p_c, vc_ref[:, sl], preferred_element_type=F32))
            o_ref[:, sl] = (o / l).astype(o_ref.dtype)
    st_ref[...] = stats


def _dilated_attention_group(qkv_g, B, S, dil, n_heads):
    hw = n_heads * HEAD_DIM
    L = S // dil
    nb = L // ATTN_BLOCK
    a = qkv_g.reshape(B, dil, L, 3 * hw)
    blk = (None, None, ATTN_BLOCK, hw)
    cur = lambda c: (lambda b, r, n: (b, r, n, c))
    prev = lambda c: (lambda b, r, n: (b, r, jnp.maximum(n - 1, 0), c))
    return pl.pallas_call(
        functools.partial(_dil_attn_kernel, n_heads=n_heads, scale=1.0 / math.sqrt(HEAD_DIM)),
        grid=(B, dil, nb),
        in_specs=[pl.BlockSpec(blk, cur(0)),
                  pl.BlockSpec(blk, prev(1)), pl.BlockSpec(blk, cur(1)),
                  pl.BlockSpec(blk, prev(2)), pl.BlockSpec(blk, cur(2))],
        out_specs=[pl.BlockSpec(blk, lambda b, r, n: (b, r, n, 0)),
                   pl.BlockSpec((None, None, ATTN_BLOCK, LANES), lambda b, r, n: (b, r, n, 0))],
        out_shape=[jax.ShapeDtypeStruct((B, dil, L, hw), BF16),
                   jax.ShapeDtypeStruct((B, dil, L, LANES), F32)],
        compiler_params=_params("parallel", "parallel", "arbitrary"),
        name=f"dilated_attn_d{dil}",
    )(a, a, a, a, a)


def _merge_kernel(*refs, dils, n_heads):
    G = len(dils)
    o_refs, s_refs, out_ref = refs[:G], refs[G:2 * G], refs[2 * G]
    tm, hw = out_ref.shape
    o, s = [], []
    for o_ref, s_ref, dil in zip(o_refs, s_refs, dils):
        o_g = o_ref[...].reshape(tm, hw)
        s_g = s_ref[...].reshape(tm, LANES)
        if dil == 1:
            o_g = o_g.astype(F32)
        else:
            pinv = _residue_perm(tm, dil, inverse=True)
            o_g = jnp.dot(pinv.astype(BF16), o_g, preferred_element_type=F32)
            s_g = jnp.dot(pinv.astype(F32), s_g, preferred_element_type=F32,
                          precision=lax.Precision.HIGHEST)
        o.append(o_g)
        s.append(s_g)
    m = functools.reduce(jnp.maximum, s)
    e = [jnp.exp(x - m) for x in s]
    den = functools.reduce(lambda a, b: a + b, e)
    w = [x / den for x in e]
    for h in range(n_heads):
        sl = slice(h * HEAD_DIM, (h + 1) * HEAD_DIM)
        acc = w[0][:, h:h + 1] * o[0][:, sl]
        for g in range(1, G):
            acc = acc + w[g][:, h:h + 1] * o[g][:, sl]
        out_ref[:, sl] = acc.astype(out_ref.dtype)


def _merge_groups(os, sts, dils, B, S, n_heads, tm=256):
    hw = n_heads * HEAD_DIM
    bpb = S // tm
    spec = lambda d, w: pl.BlockSpec((None, d, tm // d, w), lambda i: (i // bpb, 0, i % bpb, 0))
    return pl.pallas_call(
        functools.partial(_merge_kernel, dils=tuple(dils), n_heads=n_heads),
        grid=(B * bpb,),
        in_specs=[spec(d, hw) for d in dils] + [spec(d, LANES) for d in dils],
        out_specs=pl.BlockSpec((tm, hw), lambda i: (i, 0)),
        out_shape=jax.ShapeDtypeStruct((B * S, hw), BF16),
        compiler_params=_params("parallel"),
        name="merge_groups",
    )(*os, *sts)


def _sb_kernel(q_ref, k_ref, v_ref, o_ref, *, scale):
    tq = SB_BLOCK
    row = lax.broadcasted_iota(jnp.int32, (tq, tq), 0)
    col = lax.broadcasted_iota(jnp.int32, (tq, tq), 1)
    strict = col < row
    tri = (row > col).astype(BF16)

    def tiles(q, first, n, carry, diag_first):
        acc, run = carry
        starts = [pl.multiple_of((first - u) * tq, tq) for u in range(n)]
        zs = [lax.dot_general(q, k_ref[pl.ds(s, tq), :], _NT, preferred_element_type=F32) * scale
              for s in starts]
        lks, lss = [], []
        for u, z in enumerate(zs):
            lk = -(jnp.maximum(z, 0.0) + jnp.log(1.0 + jnp.exp(-jnp.abs(z))))
            lss.append(z + lk)
            if diag_first and u == 0:
                lk = jnp.where(strict, lk, 0.0)
            lks.append(lk)
        sufs = [jnp.dot(lk.astype(BF16), tri, preferred_element_type=F32) for lk in lks]
        ps = []
        for u in range(n):
            a = jnp.exp(lss[u] + (sufs[u] + run))
            if diag_first and u == 0:
                a = jnp.where(strict, a, 0.0)
            ps.append(a.astype(BF16))
            run = run + jnp.sum(lks[u], axis=1, keepdims=True)
        for u in range(n):
            acc = acc + jnp.dot(ps[u], v_ref[pl.ds(starts[u], tq), :], preferred_element_type=F32)
        return acc, run

    def q_tile(qi, _):
        rows = pl.ds(pl.multiple_of(qi * tq, tq), tq)
        q = q_ref[rows, :]
        carry = (jnp.zeros((tq, HEAD_DIM), F32), jnp.zeros((tq, 1), F32))
        acc, run = lax.cond(qi == 0, functools.partial(tiles, q, qi, 1, diag_first=True),
                            functools.partial(tiles, q, qi, 2, diag_first=True), carry)
        alive = lambda r: (jnp.max(r) > SB_DEAD).astype(jnp.int32)
        rest = jnp.maximum(qi - 1, 0)
        rem = rest % SB_UNROLL

        def walk(n, first, count, state):
            def body(c):
                t, acc, run, _ = c
                acc, run = tiles(q, first - t * n, n, (acc, run), False)
                return t + 1, acc, run, alive(run)
            return lax.while_loop(lambda c: (c[0] < count) & (c[3] != 0), body, (0,) + state)[1:]

        state = (acc, run, alive(run))
        state = walk(1, rest - 1, rem, state)
        state = walk(SB_UNROLL, rest - 1 - rem, rest // SB_UNROLL, state)
        o_ref[rows, :] = state[0].astype(o_ref.dtype)
        return 0

    lax.fori_loop(0, q_ref.shape[0] // tq, q_tile, 0)


def _stick_breaking(q, kv, n_heads):
    B, _, S, _ = q.shape
    head = lambda off: pl.BlockSpec((None, None, S, HEAD_DIM), lambda b, h: (b, off + h, 0, 0))
    out = pl.pallas_call(
        functools.partial(_sb_kernel, scale=1.0 / math.sqrt(HEAD_DIM)),
        grid=(B, n_heads),
        in_specs=[head(0), head(0), head(n_heads)],
        out_specs=pl.BlockSpec((None, S, HEAD_DIM), lambda b, h: (b, 0, h)),
        out_shape=jax.ShapeDtypeStruct((B, S, n_heads * HEAD_DIM), BF16),
        compiler_params=_params("parallel", "arbitrary"),
        name="stick_breaking",
    )(q, kv, kv)
    return out.reshape(B * S, n_heads * HEAD_DIM)


def _swiglu_kernel(te_ref, tv_ref, x_ref, wg_ref, wu_ref, wd_ref, sc_ref, o_ref, *, use_scale):
    t = pl.program_id(0)
    f = pl.program_id(1)
    tm = x_ref.shape[0]
    steps = (tv_ref[t] + ROW_STEP - 1) // ROW_STEP

    def body(m):
        x = x_ref[:m, :]
        g = jnp.dot(x, wg_ref[...].astype(BF16), preferred_element_type=F32)
        u = jnp.dot(x, wu_ref[...].astype(BF16), preferred_element_type=F32)
        mid = (g * _sigmoid(g) * u).astype(BF16)

        @pl.when(f == 0)
        def _():
            o_ref[...] = jnp.zeros_like(o_ref)

        for c in range(0, o_ref.shape[1], DOWN_CHUNK):
            sl = slice(c, c + DOWN_CHUNK)
            o_ref[:m, sl] += jnp.dot(mid, wd_ref[:, sl].astype(BF16), preferred_element_type=F32)

        if use_scale:
            @pl.when(f == pl.num_programs(1) - 1)
            def _():
                o_ref[:m, :] *= sc_ref[:m, :]

    for k in range(1, tm // ROW_STEP + 1):
        pl.when(steps == k)(functools.partial(body, k * ROW_STEP))

    @pl.when((steps == 0) & (f == 0))
    def _():
        o_ref[...] = jnp.zeros_like(o_ref)


def _swiglu(x, w_gate, w_up, w_down, tile_expert, tile_rows, row_scale, tm=1024, tf=256):
    P, D = x.shape
    E, _, F = w_gate.shape
    nf = F // tf
    use_scale = row_scale is not None
    if not use_scale:
        row_scale = jnp.ones((P, 1), F32)
    fidx = lambda t, f, tv: jnp.where(tv[t] != 0, f, nf - 1)
    grid_spec = pltpu.PrefetchScalarGridSpec(
        num_scalar_prefetch=2,
        grid=(P // tm, nf),
        in_specs=[pl.BlockSpec((tm, D), lambda t, f, te, tv: (t, 0), pipeline_mode=pl.Buffered(1)),
                  pl.BlockSpec((None, D, tf), lambda t, f, te, tv: (te[t], 0, fidx(t, f, tv))),
                  pl.BlockSpec((None, D, tf), lambda t, f, te, tv: (te[t], 0, fidx(t, f, tv))),
                  pl.BlockSpec((None, tf, D), lambda t, f, te, tv: (te[t], fidx(t, f, tv), 0)),
                  pl.BlockSpec((tm, 1), lambda t, f, te, tv: (t, 0))],
        out_specs=pl.BlockSpec((tm, D), lambda t, f, te, tv: (t, 0), pipeline_mode=pl.Buffered(1)),
    )
    return pl.pallas_call(
        functools.partial(_swiglu_kernel, use_scale=use_scale),
        grid_spec=grid_spec,
        out_shape=jax.ShapeDtypeStruct((P, D), F32),
        compiler_params=_params("parallel", "arbitrary"),
        name="swiglu",
    )(tile_expert, tile_rows, x, w_gate, w_up, w_down, row_scale)


def _router_kernel(h_ref, g_ref, w_ref, xc_ref, idx_ref, p_ref, *, n_experts):
    h = h_ref[...]
    tm, D = h.shape
    x = h * lax.rsqrt(jnp.mean(h * h, axis=-1, keepdims=True) + EPS) * g_ref[...]
    C = D // LANES
    pitch = _chunk_pitch(D)
    for c in range(C):
        xc_ref[pl.ds(c, tm, stride=pitch), :] = x[:, c * LANES:(c + 1) * LANES]
    for c in range(C, pitch):
        xc_ref[pl.ds(c, tm, stride=pitch), :] = jnp.zeros((tm, LANES), F32)
    logits = jnp.dot(x, w_ref[...], preferred_element_type=F32, precision=lax.Precision.HIGHEST)
    lane = lax.broadcasted_iota(jnp.int32, logits.shape, 1)
    logits = jnp.where(lane < n_experts, logits, -jnp.inf)
    m1 = jnp.max(logits, axis=-1, keepdims=True)
    i1 = jnp.min(jnp.where(logits == m1, lane, LANES), axis=-1, keepdims=True)
    rest = jnp.where(lane == i1, -jnp.inf, logits)
    m2 = jnp.max(rest, axis=-1, keepdims=True)
    i2 = jnp.min(jnp.where(rest == m2, lane, LANES), axis=-1, keepdims=True)
    e = jnp.exp(m2 - m1)
    p1 = 1.0 / (1.0 + e)
    p2 = e / (1.0 + e)
    idx_ref[...] = jnp.where(lane == 0, i1, jnp.where(lane == 1, i2, 0))
    p_ref[...] = jnp.where(lane == 0, p1, jnp.where(lane == 1, p2, 0.0))


def _chunk_pitch(D):
    return D // LANES + SUBLANES


def _norm_router(h, gain, w_router, tm=256):
    T, D = h.shape
    E = w_router.shape[1]
    C = _chunk_pitch(D)
    w = jnp.pad(w_router, ((0, 0), (0, LANES - E)))
    xc, idx, prob = pl.pallas_call(
        functools.partial(_router_kernel, n_experts=E),
        grid=(T // tm,),
        in_specs=[pl.BlockSpec((tm, D), lambda i: (i, 0)),
                  pl.BlockSpec((1, D), lambda i: (0, 0)),
                  pl.BlockSpec((D, LANES), lambda i: (0, 0))],
        out_specs=[pl.BlockSpec((tm * C, LANES), lambda i: (i, 0)),
                   pl.BlockSpec((tm, LANES), lambda i: (i, 0)),
                   pl.BlockSpec((tm, LANES), lambda i: (i, 0))],
        out_shape=[jax.ShapeDtypeStruct((T * C, LANES), F32),
                   jax.ShapeDtypeStruct((T, LANES), jnp.int32),
                   jax.ShapeDtypeStruct((T, LANES), F32)],
        compiler_params=_params("parallel"),
        name="norm_router",
    )(h, gain.reshape(1, D), w)
    return xc, idx[:, :TOP_K], prob[:, :TOP_K]


def _route_tables(top_i, top_p, tm, n_tiles):
    T = top_i.shape[0]
    A = T * TOP_K
    flat_e = top_i.reshape(A)
    flat_p = top_p.reshape(A)
    order = jnp.argsort(flat_e, stable=True).astype(jnp.int32)
    counts = jnp.sum(flat_e[:, None] == jnp.arange(N_EXPERTS, dtype=jnp.int32)[None, :], axis=0,
                     dtype=jnp.int32)
    tiles_per = (counts + tm - 1) // tm
    tile_end = jnp.cumsum(tiles_per)
    row_start = (tile_end - tiles_per) * tm
    sorted_start = jnp.cumsum(counts) - counts
    e_sorted = flat_e[order]
    pos_sorted = row_start[e_sorted] + jnp.arange(A, dtype=jnp.int32) - sorted_start[e_sorted]
    P = n_tiles * tm
    row_token = jnp.zeros((P,), jnp.int32).at[pos_sorted].set(order // TOP_K)
    row_scale = jnp.zeros((P,), F32).at[pos_sorted].set(flat_p[order])
    pos = jnp.zeros((A,), jnp.int32).at[order].set(pos_sorted).reshape(T, TOP_K)
    tile_ids = jnp.arange(n_tiles, dtype=jnp.int32)
    used = tile_end[-1]
    te = jnp.sum(tile_ids[:, None] >= tile_end[None, :], axis=1, dtype=jnp.int32)
    te = jnp.minimum(te, N_EXPERTS - 1)
    left = counts[te] - (tile_ids - (tile_end - tiles_per)[te]) * tm
    tile_rows = jnp.where(tile_ids < used, jnp.clip(left, 0, tm), 0).astype(jnp.int32)
    last = jnp.sum((used - 1) >= tile_end, dtype=jnp.int32)
    tile_expert = jnp.where(tile_rows != 0, te, jnp.minimum(last, N_EXPERTS - 1))
    return row_token, row_scale.reshape(P, 1), pos, tile_expert, tile_rows


def _gather_kernel(tok_ref, x_hbm, o_ref, buf, sem):
    tm, D = o_ref.shape
    C = D // LANES
    pitch = _chunk_pitch(D)
    t = pl.program_id(0)

    def row_copy(tile, r):
        slot = tile % 2
        src = pl.multiple_of(tok_ref[tile * tm + r] * pitch, SUBLANES)
        dst = pl.multiple_of(r * pitch, SUBLANES)
        return pltpu.make_async_copy(x_hbm.at[pl.ds(src, C)], buf.at[slot, pl.ds(dst, C)], sem.at[slot])

    def start_tile(tile):
        def body(r8, c):
            for j in range(SUBLANES):
                row_copy(tile, r8 * SUBLANES + j).start(priority=j % 2)
            return c
        lax.fori_loop(0, tm // SUBLANES, body, 0)

    @pl.when(t == 0)
    def _():
        start_tile(0)

    @pl.when(t + 1 < pl.num_programs(0))
    def _():
        start_tile(t + 1)

    def wait(r, c):
        row_copy(t, r).wait()
        return c

    lax.fori_loop(0, tm, wait, 0, unroll=8)
    slot = t % 2
    for c in range(C):
        o_ref[:, c * LANES:(c + 1) * LANES] = buf[slot, pl.ds(c, tm, stride=pitch), :].astype(o_ref.dtype)


def _gather_rows(xc, row_token, D, tm=256):
    P = row_token.shape[0]
    grid_spec = pltpu.PrefetchScalarGridSpec(
        num_scalar_prefetch=1,
        grid=(P // tm,),
        in_specs=[pl.BlockSpec(memory_space=pl.ANY)],
        out_specs=pl.BlockSpec((tm, D), lambda t, tok: (t, 0)),
        scratch_shapes=[pltpu.VMEM((2, tm * _chunk_pitch(D), LANES), F32), pltpu.SemaphoreType.DMA((2,))],
    )
    return pl.pallas_call(
        _gather_kernel,
        grid_spec=grid_spec,
        out_shape=jax.ShapeDtypeStruct((P, D), BF16),
        compiler_params=_params("arbitrary"),
        name="gather_rows",
    )(row_token, xc)


def _combine_kernel(p0_ref, p1_ref, y_hbm, o_ref, buf0, buf1, sem):
    tm = buf0.shape[0]
    base = pl.program_id(0) * tm

    def copies(r):
        return (pltpu.make_async_copy(y_hbm.at[pl.ds(p0_ref[base + r], 1)], buf0.at[pl.ds(r, 1)], sem),
                pltpu.make_async_copy(y_hbm.at[pl.ds(p1_ref[base + r], 1)], buf1.at[pl.ds(r, 1)], sem))

    def start(r, c):
        for j, cp in enumerate(copies(r)):
            cp.start(priority=j)
        return c

    def wait(r, c):
        for cp in copies(r):
            cp.wait()
        return c

    lax.fori_loop(0, tm, start, 0)
    lax.fori_loop(0, tm, wait, 0)
    o_ref[...] = buf0[...] + buf1[...]


def _combine_rows(y, pos, tm=256):
    T = pos.shape[0]
    D = y.shape[1]
    grid_spec = pltpu.PrefetchScalarGridSpec(
        num_scalar_prefetch=2,
        grid=(T // tm,),
        in_specs=[pl.BlockSpec(memory_space=pl.ANY)],
        out_specs=pl.BlockSpec((tm, D), lambda t, p0, p1: (t, 0)),
        scratch_shapes=[pltpu.VMEM((tm, D), F32), pltpu.VMEM((tm, D), F32),
                        pltpu.SemaphoreType.DMA(())],
    )
    return pl.pallas_call(
        _combine_kernel,
        grid_spec=grid_spec,
        out_shape=jax.ShapeDtypeStruct((T, D), F32),
        compiler_params=_params("arbitrary"),
        name="combine_rows",
    )(pos[:, 0], pos[:, 1], y)


def _ple_kernel(h_ref, d_ref, p_ref, g_ref, wgd_ref, wgu_ref, wup_ref, *rest, n_norm):
    gn_refs = rest[:n_norm]
    o_ref = rest[n_norm]
    on_refs = rest[n_norm + 1:]
    h = h_ref[...] + d_ref[...]
    y = h * lax.rsqrt(jnp.mean(h * h, axis=-1, keepdims=True) + EPS) * g_ref[...]
    t = jnp.dot(y.astype(BF16), wgd_ref[...], preferred_element_type=F32)
    gate = _sigmoid(jnp.dot(t.astype(BF16), wgu_ref[...], preferred_element_type=F32))
    up = jnp.dot(p_ref[...].astype(BF16), wup_ref[...], preferred_element_type=F32)
    h = h + gate * up
    o_ref[...] = h
    if n_norm:
        y = h * lax.rsqrt(jnp.mean(h * h, axis=-1, keepdims=True) + EPS)
        for gn_ref, on_ref in zip(gn_refs, on_refs):
            on_ref[...] = (y * gn_ref[...]).astype(on_ref.dtype)


def _ple(h, delta, p_i, g_norm, w_up, w_gdown, w_gup, next_gains, tm=256):
    T, D = h.shape
    R = p_i.shape[1]
    row = pl.BlockSpec((tm, D), lambda i: (i, 0))
    gain = pl.BlockSpec((1, D), lambda i: (0, 0))
    n_norm = len(next_gains)
    outs = pl.pallas_call(
        functools.partial(_ple_kernel, n_norm=n_norm),
        grid=(T // tm,),
        in_specs=[row, row, pl.BlockSpec((tm, R), lambda i: (i, 0)), gain,
                  pl.BlockSpec((D, R), lambda i: (0, 0)),
                  pl.BlockSpec((R, D), lambda i: (0, 0)),
                  pl.BlockSpec((R, D), lambda i: (0, 0))] + [gain] * n_norm,
        out_specs=[row] * (1 + n_norm),
        out_shape=[jax.ShapeDtypeStruct((T, D), F32)] + [jax.ShapeDtypeStruct((T, D), BF16)] * n_norm,
        compiler_params=_params("parallel"),
        name="per_layer_embedding",
    )(h, delta, p_i, g_norm.reshape(1, D), w_gdown.astype(BF16), w_gup.astype(BF16),
      w_up.astype(BF16), *[g.reshape(1, D) for g in next_gains])
    return list(outs)


def kernel(x, p, norm_mix, norm_ffn, norm_ple, a_w_qkv, a_q_norm, a_k_norm, a_w_o, kv_norm, w_kv, b_w_q, b_w_o, ffn_w_gate, ffn_w_up, ffn_w_down, moe_w_router, moe_w_gate, moe_w_up, moe_w_down, ple_w_up, ple_w_gdown, ple_w_gup):
    B, S, D = x.shape
    T = B * S
    assert p.shape[0] == 2 and a_w_qkv.shape[0] == 1 and b_w_q.shape[0] == 1
    G = len(A_GROUPS)
    a_heads = a_w_o.shape[1] // HEAD_DIM
    b_heads = b_w_q.shape[2] // HEAD_DIM
    x2 = x.reshape(T, D)
    p2 = p.reshape(2, T, p.shape[-1])

    dils = [dil for _, dil in A_GROUPS]
    hns = _rmsnorm_dilated(x2, norm_mix[0], B, S, dils)
    os, sts = [], []
    for g, (span, dil) in enumerate(A_GROUPS):
        assert span // dil == ATTN_BLOCK
        qkv_g = _qkv_proj(hns[g].reshape(T, D), a_w_qkv[0], a_q_norm[0], a_k_norm[0], S, dil, g, G)
        o_g, st_g = _dilated_attention_group(qkv_g, B, S, dil, a_heads)
        os.append(o_g)
        sts.append(st_g)
    attn = _merge_groups(os, sts, dils, B, S, a_heads)
    h = _matmul(attn, a_w_o[0], F32, residual=x2, tn=1024)

    (hn,) = _rmsnorm(h, None, [norm_ffn[0]], [BF16])
    tm = 1024
    delta = _swiglu(hn, ffn_w_gate, ffn_w_up, ffn_w_down, jnp.zeros((T // tm,), jnp.int32),
                    jnp.full((T // tm,), tm, jnp.int32), None, tm=tm)
    h, hn_kv, hn_q = _ple(h, delta, p2[0], norm_ple[0], ple_w_up[0], ple_w_gdown[0], ple_w_gup[0],
                          [kv_norm, norm_mix[1]])

    kv = _matmul_heads(hn_kv, w_kv, B, S)
    q = _matmul_heads(hn_q, b_w_q[0], B, S)
    sb = _stick_breaking(q, kv, b_heads)
    h = _matmul(sb, b_w_o[0], F32, residual=h)

    xc, top_i, top_p = _norm_router(h, norm_ffn[1], moe_w_router[0])
    n_tiles = (T * TOP_K) // tm + N_EXPERTS
    row_token, row_scale, pos, tile_expert, tile_rows = _route_tables(top_i, top_p, tm, n_tiles)
    xs = _gather_rows(xc, row_token, D)
    ys = _swiglu(xs, moe_w_gate[0], moe_w_up[0], moe_w_down[0], tile_expert, tile_rows, row_scale, tm=tm)
    delta = _combine_rows(ys, pos)
    (h,) = _ple(h, delta, p2[1], norm_ple[1], ple_w_up[1], ple_w_gdown[1], ple_w_gup[1], [])
    return h.reshape(B, S, D)
```

```python
import functools
import math

import jax
import jax.numpy as jnp
from jax import lax
from jax.experimental import pallas as pl
from jax.experimental.pallas import tpu as pltpu

F32 = jnp.float32
BF16 = jnp.bfloat16

EPS = 1e-6
HEAD_DIM = 128
ROPE_THETA = 10000.0
A_GROUPS = ((128, 1), (512, 4), (2048, 16))
N_EXPERTS = 8
TOP_K = 2

LANES = 128
SUBLANES = 8
ATTN_BLOCK = 128
ATTN_HEAD_GROUP = 4
SB_BLOCK = 256
SB_UNROLL = 4
SB_DEAD = -104.0
ROW_STEP = 256
DOWN_CHUNK = 512
VMEM_LIMIT = 56 * 1024 * 1024

_NT = (((1,), (1,)), ((), ()))


def _params(*semantics, vmem_limit=VMEM_LIMIT):
    return pltpu.CompilerParams(dimension_semantics=semantics, vmem_limit_bytes=vmem_limit)


def _sigmoid(x):
    return 1.0 / (1.0 + jnp.exp(-x))


def _residue_perm(n_rows, dil, inverse):
    n = n_rows // dil
    assert dil & (dil - 1) == 0 and n & (n - 1) == 0
    out_row = lax.broadcasted_iota(jnp.int32, (n_rows, n_rows), 0)
    in_row = lax.broadcasted_iota(jnp.int32, (n_rows, n_rows), 1)
    if inverse:
        src = (out_row & (dil - 1)) * n + (out_row >> (dil.bit_length() - 1))
    else:
        src = (out_row & (n - 1)) * dil + (out_row >> (n.bit_length() - 1))
    return in_row == src


def _norm_kernel(*refs, has_delta, n_out):
    refs = list(refs)
    h_ref = refs.pop(0)
    d_ref = refs.pop(0) if has_delta else None
    g_refs = [refs.pop(0) for _ in range(n_out)]
    hsum_ref = refs.pop(0) if has_delta else None
    o_refs = refs
    h = h_ref[...]
    if has_delta:
        h = h + d_ref[...]
        hsum_ref[...] = h
    y = h * lax.rsqrt(jnp.mean(h * h, axis=-1, keepdims=True) + EPS)
    for g_ref, o_ref in zip(g_refs, o_refs):
        o_ref[...] = (y * g_ref[...]).astype(o_ref.dtype)


def _rmsnorm(h, delta, gains, out_dtypes, tm=256):
    T, D = h.shape
    has_delta = delta is not None
    row = pl.BlockSpec((tm, D), lambda i: (i, 0))
    gain = pl.BlockSpec((1, D), lambda i: (0, 0))
    ins = [h] + ([delta] if has_delta else []) + [g.reshape(1, D) for g in gains]
    in_specs = [row] * (2 if has_delta else 1) + [gain] * len(gains)
    out_shape = ([jax.ShapeDtypeStruct((T, D), F32)] if has_delta else []) + [
        jax.ShapeDtypeStruct((T, D), dt) for dt in out_dtypes]
    outs = pl.pallas_call(
        functools.partial(_norm_kernel, has_delta=has_delta, n_out=len(gains)),
        grid=(T // tm,),
        in_specs=in_specs,
        out_specs=[row] * len(out_shape),
        out_shape=out_shape,
        compiler_params=_params("parallel"),
        name="rmsnorm",
    )(*ins)
    return list(outs)


def _norm_dilated_kernel(h_ref, g_ref, *o_refs, dils):
    h = h_ref[...]
    tm = h.shape[0]
    y = (h * lax.rsqrt(jnp.mean(h * h, axis=-1, keepdims=True) + EPS) * g_ref[...]).astype(BF16)
    for o_ref, dil in zip(o_refs, dils):
        if dil == 1:
            o_ref[0, 0] = y
        else:
            perm = _residue_perm(tm, dil, inverse=False).astype(BF16)
            yp = jnp.dot(perm, y, preferred_element_type=F32).astype(BF16)
            n = tm // dil
            for r in range(dil):
                o_ref[0, r] = yp[r * n:(r + 1) * n]


def _rmsnorm_dilated(h, gain, B, S, dils, tm=256):
    T, D = h.shape
    bpb = S // tm
    return pl.pallas_call(
        functools.partial(_norm_dilated_kernel, dils=tuple(dils)),
        grid=(T // tm,),
        in_specs=[pl.BlockSpec((tm, D), lambda i: (i, 0)), pl.BlockSpec((1, D), lambda i: (0, 0))],
        out_specs=[pl.BlockSpec((1, d, tm // d, D), lambda i: (i // bpb, 0, i % bpb, 0)) for d in dils],
        out_shape=[jax.ShapeDtypeStruct((B, d, S // d, D), BF16) for d in dils],
        compiler_params=_params("parallel"),
        name="rmsnorm_dilated",
    )(h, gain.reshape(1, D))


def _mm_kernel(x_ref, w_ref, *rest, has_res):
    acc = jnp.dot(x_ref[...], w_ref[...].astype(BF16), preferred_element_type=F32)
    if has_res:
        r_ref, o_ref = rest
        acc = r_ref[...] + acc
    else:
        (o_ref,) = rest
    o_ref[...] = acc.astype(o_ref.dtype)


def _matmul(x, w, out_dtype, residual=None, tm=1024, tn=512):
    M, K = x.shape
    N = w.shape[1]
    has_res = residual is not None
    in_specs = [pl.BlockSpec((tm, K), lambda i, j: (i, 0)),
                pl.BlockSpec((K, tn), lambda i, j: (0, j))]
    ins = [x, w]
    if has_res:
        in_specs.append(pl.BlockSpec((tm, tn), lambda i, j: (i, j)))
        ins.append(residual)
    return pl.pallas_call(
        functools.partial(_mm_kernel, has_res=has_res),
        grid=(M // tm, N // tn),
        in_specs=in_specs,
        out_specs=pl.BlockSpec((tm, tn), lambda i, j: (i, j)),
        out_shape=jax.ShapeDtypeStruct((M, N), out_dtype),
        compiler_params=_params("parallel", "arbitrary"),
        name="matmul",
    )(*ins)


def _mm_heads_kernel(x_ref, w_ref, o_ref):
    acc = jnp.dot(x_ref[...], w_ref[...].astype(BF16), preferred_element_type=F32)
    for hh in range(o_ref.shape[0]):
        o_ref[hh] = acc[:, hh * HEAD_DIM:(hh + 1) * HEAD_DIM].astype(o_ref.dtype)


def _matmul_heads(x, w, B, S, tm=1024, tn=512):
    M, K = x.shape
    N = w.shape[1]
    bpb = S // tm
    hpt = tn // HEAD_DIM
    return pl.pallas_call(
        _mm_heads_kernel,
        grid=(M // tm, N // tn),
        in_specs=[pl.BlockSpec((tm, K), lambda i, j: (i, 0)),
                  pl.BlockSpec((K, tn), lambda i, j: (0, j))],
        out_specs=pl.BlockSpec((None, hpt, tm, HEAD_DIM), lambda i, j: (i // bpb, j, i % bpb, 0)),
        out_shape=jax.ShapeDtypeStruct((B, N // HEAD_DIM, S, HEAD_DIM), BF16),
        compiler_params=_params("parallel", "arbitrary"),
        name="matmul_heads",
    )(x, w)


def _qkv_kernel(x_ref, w_ref, cos_ref, sin_ref, gq_ref, gk_ref, o_ref, acc_a, acc_b, *,
                q_tiles, qk_tiles, n_tiles):
    j = pl.program_id(1)

    def matmul(acc_ref):
        acc_ref[...] = jnp.dot(x_ref[...], w_ref[...].astype(BF16), preferred_element_type=F32)

    def finish(acc_ref):
        jp = j - 1
        is_qk = jp < qk_tiles
        gain = jnp.where(jp < q_tiles, gq_ref[...], gk_ref[...])
        cos = cos_ref[...]
        sin = sin_ref[...]
        for hh in range(acc_ref.shape[1] // HEAD_DIM):
            sl = slice(hh * HEAD_DIM, (hh + 1) * HEAD_DIM)
            blk = acc_ref[:, sl]
            y = blk * lax.rsqrt(jnp.mean(blk * blk, axis=-1, keepdims=True) + EPS) * gain
            y = y * cos + pltpu.roll(y, HEAD_DIM // 2, 1) * sin
            o_ref[:, sl] = jnp.where(is_qk, y, blk).astype(o_ref.dtype)

    even = j % 2 == 0

    @pl.when(j == 0)
    def _():
        matmul(acc_a)

    @pl.when((j > 0) & (j < n_tiles) & even)
    def _():
        matmul(acc_a)
        finish(acc_b)

    @pl.when((j < n_tiles) & jnp.logical_not(even))
    def _():
        matmul(acc_b)
        finish(acc_a)

    @pl.when(j == n_tiles)
    def _():
        finish(acc_b if n_tiles % 2 == 0 else acc_a)


def _rope_tables(S, dil):
    half = HEAD_DIM // 2
    inv = ROPE_THETA ** (-jnp.arange(half, dtype=F32) / half)
    pos = jnp.arange(S, dtype=F32).reshape(S // dil, dil).T.reshape(S)
    ang = pos[:, None] * inv[None, :]
    cos, sin = jnp.cos(ang), jnp.sin(ang)
    return jnp.concatenate([cos, cos], axis=-1), jnp.concatenate([-sin, sin], axis=-1)


def _qkv_proj(x, w, g_q, g_k, S, dil, group, n_groups, tm=1024, tn=512):
    M, K = x.shape
    hw = w.shape[1] // (3 * n_groups)
    tpg = hw // tn
    cos, sin = _rope_tables(S, dil)
    pos_blocks = S // tm
    n_tiles = 3 * tpg

    def w_col(i, j):
        jc = jnp.minimum(j, n_tiles - 1)
        return (0, ((jc // tpg) * n_groups + group) * tpg + jc % tpg)

    return pl.pallas_call(
        functools.partial(_qkv_kernel, q_tiles=tpg, qk_tiles=2 * tpg, n_tiles=n_tiles),
        grid=(M // tm, n_tiles + 1),
        in_specs=[pl.BlockSpec((tm, K), lambda i, j: (i, 0)),
                  pl.BlockSpec((K, tn), w_col),
                  pl.BlockSpec((tm, HEAD_DIM), lambda i, j: (i % pos_blocks, 0)),
                  pl.BlockSpec((tm, HEAD_DIM), lambda i, j: (i % pos_blocks, 0)),
                  pl.BlockSpec((1, HEAD_DIM), lambda i, j: (0, 0)),
                  pl.BlockSpec((1, HEAD_DIM), lambda i, j: (0, 0))],
        out_specs=pl.BlockSpec((tm, tn), lambda i, j: (i, jnp.maximum(j - 1, 0))),
        out_shape=jax.ShapeDtypeStruct((M, 3 * hw), BF16),
        scratch_shapes=[pltpu.VMEM((tm, tn), F32), pltpu.VMEM((tm, tn), F32)],
        compiler_params=_params("parallel", "arbitrary"),
        name=f"qkv_proj_d{dil}",
    )(x, w, cos, sin, g_q.reshape(1, HEAD_DIM), g_k.reshape(1, HEAD_DIM))


def _dil_attn_kernel(q_ref, kp_ref, kc_ref, vp_ref, vc_ref, o_ref, st_ref, *, n_heads, scale):
    n = pl.program_id(2)
    blk = q_ref.shape[0]
    i = lax.broadcasted_iota(jnp.int32, (blk, blk), 0)
    j = lax.broadcasted_iota(jnp.int32, (blk, blk), 1)
    mask_p = (j >= i) & (n > 0)
    mask_c = j <= i
    lane = lax.broadcasted_iota(jnp.int32, (blk, LANES), 1)
    stats = jnp.zeros((blk, LANES), F32)
    for h0 in range(0, n_heads, ATTN_HEAD_GROUP):
        sls = [slice(h * HEAD_DIM, (h + 1) * HEAD_DIM) for h in range(h0, h0 + ATTN_HEAD_GROUP)]
        s_ps = [lax.dot_general(q_ref[:, sl], kp_ref[:, sl], _NT, preferred_element_type=F32) for sl in sls]
        s_cs = [lax.dot_general(q_ref[:, sl], kc_ref[:, sl], _NT, preferred_element_type=F32) for sl in sls]
        p_ps, p_cs, ls = [], [], []
        for u, (s_p, s_c) in enumerate(zip(s_ps, s_cs)):
            s_p = jnp.where(mask_p, s_p * scale, -jnp.inf)
            s_c = jnp.where(mask_c, s_c * scale, -jnp.inf)
            m = jnp.maximum(jnp.max(s_p, axis=-1, keepdims=True), jnp.max(s_c, axis=-1, keepdims=True))
            p_p = jnp.exp(s_p - m)
            p_c = jnp.exp(s_c - m)
            l = jnp.sum(p_p, axis=-1, keepdims=True) + jnp.sum(p_c, axis=-1, keepdims=True)
            p_ps.append(p_p.astype(BF16))
            p_cs.append(p_c.astype(BF16))
            ls.append(l)
            stats = jnp.where(lane == h0 + u, m + jnp.log(l), stats)
        for sl, p_p, p_c, l in zip(sls, p_ps, p_cs, ls):
            o = (jnp.dot(p_p, vp_ref[:, sl], preferred_element_type=F32)
                 + jnp.dot(p_c, vc_ref[:, sl], preferred_element_type=F32))
            o_ref[:, sl] = (o / l).astype(o_ref.dtype)
    st_ref[...] = stats


def _dilated_attention_group(qkv_g, B, S, dil, n_heads):
    hw = n_heads * HEAD_DIM
    L = S // dil
    nb = L // ATTN_BLOCK
    a = qkv_g.reshape(B, dil, L, 3 * hw)
    blk = (None, None, ATTN_BLOCK, hw)
    cur = lambda c: (lambda b, r, n: (b, r, n, c))
    prev = lambda c: (lambda b, r, n: (b, r, jnp.maximum(n - 1, 0), c))
    return pl.pallas_call(
        functools.partial(_dil_attn_kernel, n_heads=n_heads, scale=1.0 / math.sqrt(HEAD_DIM)),
        grid=(B, dil, nb),
        in_specs=[pl.BlockSpec(blk, cur(0)),
                  pl.BlockSpec(blk, prev(1)), pl.BlockSpec(blk, cur(1)),
                  pl.BlockSpec(blk, prev(2)), pl.BlockSpec(blk, cur(2))],
        out_specs=[pl.BlockSpec(blk, lambda b, r, n: (b, r, n, 0)),
                   pl.BlockSpec((None, None, ATTN_BLOCK, LANES), lambda b, r, n: (b, r, n, 0))],
        out_shape=[jax.ShapeDtypeStruct((B, dil, L, hw), BF16),
                   jax.ShapeDtypeStruct((B, dil, L, LANES), F32)],
        compiler_params=_params("parallel", "parallel", "arbitrary"),
        name=f"dilated_attn_d{dil}",
    )(a, a, a, a, a)


def _merge_kernel(*refs, dils, n_heads):
    G = len(dils)
    o_refs, s_refs, out_ref = refs[:G], refs[G:2 * G], refs[2 * G]
    tm, hw = out_ref.shape
    o, s = [], []
    for o_ref, s_ref, dil in zip(o_refs, s_refs, dils):
        o_g = o_ref[...].reshape(tm, hw)
        s_g = s_ref[...].reshape(tm, LANES)
        if dil == 1:
            o_g = o_g.astype(F32)
        else:
            pinv = _residue_perm(tm, dil, inverse=True)
            o_g = jnp.dot(pinv.astype(BF16), o_g, preferred_element_type=F32)
            s_g = jnp.dot(pinv.astype(F32), s_g, preferred_element_type=F32,
                          precision=lax.Precision.HIGHEST)
        o.append(o_g)
        s.append(s_g)
    m = functools.reduce(jnp.maximum, s)
    e = [jnp.exp(x - m) for x in s]
    den = functools.reduce(lambda a, b: a + b, e)
    w = [x / den for x in e]
    for h in range(n_heads):
        sl = slice(h * HEAD_DIM, (h + 1) * HEAD_DIM)
        acc = w[0][:, h:h + 1] * o[0][:, sl]
        for g in range(1, G):
            acc = acc + w[g][:, h:h + 1] * o[g][:, sl]
        out_ref[:, sl] = acc.astype(out_ref.dtype)


def _merge_groups(os, sts, dils, B, S, n_heads, tm=256):
    hw = n_heads * HEAD_DIM
    bpb = S // tm
    spec = lambda d, w: pl.BlockSpec((None, d, tm // d, w), lambda i: (i // bpb, 0, i % bpb, 0))
    return pl.pallas_call(
        functools.partial(_merge_kernel, dils=tuple(dils), n_heads=n_heads),
        grid=(B * bpb,),
        in_specs=[spec(d, hw) for d in dils] + [spec(d, LANES) for d in dils],
        out_specs=pl.BlockSpec((tm, hw), lambda i: (i, 0)),
        out_shape=jax.ShapeDtypeStruct((B * S, hw), BF16),
        compiler_params=_params("parallel"),
        name="merge_groups",
    )(*os, *sts)


def _sb_kernel(q_ref, k_ref, v_ref, o_ref, *, scale):
    tq = SB_BLOCK
    row = lax.broadcasted_iota(jnp.int32, (tq, tq), 0)
    col = lax.broadcasted_iota(jnp.int32, (tq, tq), 1)
    strict = col < row
    tri = (row > col).astype(BF16)

    def tiles(qs, jobs, carries):
        starts = [pl.multiple_of(kb * tq, tq) for _, kb, _ in jobs]
        zs = [lax.dot_general(qs[j], k_ref[pl.ds(s, tq), :], _NT, preferred_element_type=F32) * scale
              for (j, _, _), s in zip(jobs, starts)]
        lks, lss = [], []
        for (_, _, diag), z in zip(jobs, zs):
            lk = -(jnp.maximum(z, 0.0) + jnp.log(1.0 + jnp.exp(-jnp.abs(z))))
            lss.append(z + lk)
            lks.append(jnp.where(strict, lk, 0.0) if diag else lk)
        sufs = [jnp.dot(lk.astype(BF16), tri, preferred_element_type=F32) for lk in lks]
        accs = [c[0] for c in carries]
        runs = [c[1] for c in carries]
        ps = []
        for u, (j, _, diag) in enumerate(jobs):
            a = jnp.exp(lss[u] + (sufs[u] + runs[j]))
            ps.append((jnp.where(strict, a, 0.0) if diag else a).astype(BF16))
            runs[j] = runs[j] + jnp.sum(lks[u], axis=1, keepdims=True)
        for u, (j, _, _) in enumerate(jobs):
            accs[j] = accs[j] + jnp.dot(ps[u], v_ref[pl.ds(starts[u], tq), :], preferred_element_type=F32)
        return tuple(zip(accs, runs))

    alive = lambda r: (jnp.max(r) > SB_DEAD).astype(jnp.int32)

    def finish(q, qi, carry):
        rest = jnp.maximum(qi - 1, 0)
        rem = rest % SB_UNROLL

        def walk(n, first, count, state):
            def body(c):
                t, acc, run, _ = c
                jobs = [(0, first - t * n - u, False) for u in range(n)]
                ((acc, run),) = tiles([q], jobs, [(acc, run)])
                return t + 1, acc, run, alive(run)
            return lax.while_loop(lambda c: (c[0] < count) & (c[3] != 0), body, (0,) + state)[1:]

        state = carry + (alive(carry[1]),)
        state = walk(1, rest - 1, rem, state)
        state = walk(SB_UNROLL, rest - 1 - rem, rest // SB_UNROLL, state)
        return state[0]

    def q_pair(p, _):
        qa, qb = 2 * p, 2 * p + 1
        rows = [pl.ds(pl.multiple_of(qi * tq, tq), tq) for qi in (qa, qb)]
        qs = [q_ref[r, :] for r in rows]
        zero = (jnp.zeros((tq, HEAD_DIM), F32), jnp.zeros((tq, 1), F32))
        first_pair = [(0, qa, True), (1, qb, True), (1, qa, False)]
        later_pair = [(0, qa, True), (0, qa - 1, False), (1, qb, True), (1, qa, False)]
        carries = lax.cond(p == 0, functools.partial(tiles, qs, first_pair),
                           functools.partial(tiles, qs, later_pair), (zero, zero))
        for q, qi, r, carry in zip(qs, (qa, qb), rows, carries):
            o_ref[r, :] = finish(q, qi, carry).astype(o_ref.dtype)
        return 0

    lax.fori_loop(0, q_ref.shape[0] // (2 * tq), q_pair, 0)


def _stick_breaking(q, kv, n_heads):
    B, _, S, _ = q.shape
    head = lambda off: pl.BlockSpec((None, None, S, HEAD_DIM), lambda b, h: (b, off + h, 0, 0))
    out = pl.pallas_call(
        functools.partial(_sb_kernel, scale=1.0 / math.sqrt(HEAD_DIM)),
        grid=(B, n_heads),
        in_specs=[head(0), head(0), head(n_heads)],
        out_specs=pl.BlockSpec((None, S, HEAD_DIM), lambda b, h: (b, 0, h)),
        out_shape=jax.ShapeDtypeStruct((B, S, n_heads * HEAD_DIM), BF16),
        compiler_params=_params("parallel", "arbitrary"),
        name="stick_breaking",
    )(q, kv, kv)
    return out.reshape(B * S, n_heads * HEAD_DIM)


def _swiglu_kernel(te_ref, tv_ref, x_ref, wg_ref, wu_ref, wd_ref, sc_ref, o_ref, *, use_scale):
    t = pl.program_id(0)
    f = pl.program_id(1)
    tm = x_ref.shape[0]
    steps = (tv_ref[t] + ROW_STEP - 1) // ROW_STEP

    def body(m):
        x = x_ref[:m, :]
        g = jnp.dot(x, wg_ref[...].astype(BF16), preferred_element_type=F32)
        u = jnp.dot(x, wu_ref[...].astype(BF16), preferred_element_type=F32)
        mid = (g * _sigmoid(g) * u).astype(BF16)

        @pl.when(f == 0)
        def _():
            o_ref[...] = jnp.zeros_like(o_ref)

        for c in range(0, o_ref.shape[1], DOWN_CHUNK):
            sl = slice(c, c + DOWN_CHUNK)
            o_ref[:m, sl] += jnp.dot(mid, wd_ref[:, sl].astype(BF16), preferred_element_type=F32)

        if use_scale:
            @pl.when(f == pl.num_programs(1) - 1)
            def _():
                o_ref[:m, :] *= sc_ref[:m, :]

    for k in range(1, tm // ROW_STEP + 1):
        pl.when(steps == k)(functools.partial(body, k * ROW_STEP))

    @pl.when((steps == 0) & (f == 0))
    def _():
        o_ref[...] = jnp.zeros_like(o_ref)


def _swiglu(x, w_gate, w_up, w_down, tile_expert, tile_rows, row_scale, tm=1024, tf=256):
    P, D = x.shape
    E, _, F = w_gate.shape
    nf = F // tf
    use_scale = row_scale is not None
    if not use_scale:
        row_scale = jnp.ones((P, 1), F32)
    fidx = lambda t, f, tv: jnp.where(tv[t] != 0, f, nf - 1)
    grid_spec = pltpu.PrefetchScalarGridSpec(
        num_scalar_prefetch=2,
        grid=(P // tm, nf),
        in_specs=[pl.BlockSpec((tm, D), lambda t, f, te, tv: (t, 0), pipeline_mode=pl.Buffered(1)),
                  pl.BlockSpec((None, D, tf), lambda t, f, te, tv: (te[t], 0, fidx(t, f, tv))),
                  pl.BlockSpec((None, D, tf), lambda t, f, te, tv: (te[t], 0, fidx(t, f, tv))),
                  pl.BlockSpec((None, tf, D), lambda t, f, te, tv: (te[t], fidx(t, f, tv), 0)),
                  pl.BlockSpec((tm, 1), lambda t, f, te, tv: (t, 0))],
        out_specs=pl.BlockSpec((tm, D), lambda t, f, te, tv: (t, 0), pipeline_mode=pl.Buffered(1)),
    )
    return pl.pallas_call(
        functools.partial(_swiglu_kernel, use_scale=use_scale),
        grid_spec=grid_spec,
        out_shape=jax.ShapeDtypeStruct((P, D), F32),
        compiler_params=_params("parallel", "arbitrary"),
        name="swiglu",
    )(tile_expert, tile_rows, x, w_gate, w_up, w_down, row_scale)


def _router_kernel(h_ref, g_ref, w_ref, xc_ref, idx_ref, p_ref, *, n_experts):
    h = h_ref[...]
    tm, D = h.shape
    x = h * lax.rsqrt(jnp.mean(h * h, axis=-1, keepdims=True) + EPS) * g_ref[...]
    C = D // LANES
    pitch = _chunk_pitch(D)
    for c in range(C):
        xc_ref[pl.ds(c, tm, stride=pitch), :] = x[:, c * LANES:(c + 1) * LANES]
    for c in range(C, pitch):
        xc_ref[pl.ds(c, tm, stride=pitch), :] = jnp.zeros((tm, LANES), F32)
    logits = jnp.dot(x, w_ref[...], preferred_element_type=F32, precision=lax.Precision.HIGHEST)
    lane = lax.broadcasted_iota(jnp.int32, logits.shape, 1)
    logits = jnp.where(lane < n_experts, logits, -jnp.inf)
    m1 = jnp.max(logits, axis=-1, keepdims=True)
    i1 = jnp.min(jnp.where(logits == m1, lane, LANES), axis=-1, keepdims=True)
    rest = jnp.where(lane == i1, -jnp.inf, logits)
    m2 = jnp.max(rest, axis=-1, keepdims=True)
    i2 = jnp.min(jnp.where(rest == m2, lane, LANES), axis=-1, keepdims=True)
    e = jnp.exp(m2 - m1)
    p1 = 1.0 / (1.0 + e)
    p2 = e / (1.0 + e)
    idx_ref[...] = jnp.where(lane == 0, i1, jnp.where(lane == 1, i2, 0))
    p_ref[...] = jnp.where(lane == 0, p1, jnp.where(lane == 1, p2, 0.0))


def _chunk_pitch(D):
    return D // LANES + SUBLANES


def _norm_router(h, gain, w_router, tm=256):
    T, D = h.shape
    E = w_router.shape[1]
    C = _chunk_pitch(D)
    w = jnp.pad(w_router, ((0, 0), (0, LANES - E)))
    xc, idx, prob = pl.pallas_call(
        functools.partial(_router_kernel, n_experts=E),
        grid=(T // tm,),
        in_specs=[pl.BlockSpec((tm, D), lambda i: (i, 0)),
                  pl.BlockSpec((1, D), lambda i: (0, 0)),
                  pl.BlockSpec((D, LANES), lambda i: (0, 0))],
        out_specs=[pl.BlockSpec((tm * C, LANES), lambda i: (i, 0)),
                   pl.BlockSpec((tm, LANES), lambda i: (i, 0)),
                   pl.BlockSpec((tm, LANES), lambda i: (i, 0))],
        out_shape=[jax.ShapeDtypeStruct((T * C, LANES), F32),
                   jax.ShapeDtypeStruct((T, LANES), jnp.int32),
                   jax.ShapeDtypeStruct((T, LANES), F32)],
        compiler_params=_params("parallel"),
        name="norm_router",
    )(h, gain.reshape(1, D), w)
    return xc, idx[:, :TOP_K], prob[:, :TOP_K]


def _route_tables(top_i, top_p, tm, n_tiles):
    T = top_i.shape[0]
    A = T * TOP_K
    flat_e = top_i.reshape(A)
    flat_p = top_p.reshape(A)
    experts = jnp.arange(N_EXPERTS, dtype=jnp.int32)
    onehot = flat_e[:, None] == experts[None, :]
    running = jnp.cumsum(onehot.astype(jnp.int32), axis=0)
    counts = running[-1]
    pick = lambda table: jnp.sum(jnp.where(onehot, table, 0), axis=1)
    rank = pick(running) - 1
    tiles_per = (counts + tm - 1) // tm
    tile_end = jnp.cumsum(tiles_per)
    row_start = (tile_end - tiles_per) * tm
    pos = (pick(row_start[None, :]) + rank).reshape(T, TOP_K)
    tile_ids = jnp.arange(n_tiles, dtype=jnp.int32)
    used = tile_end[-1]
    te = jnp.minimum(jnp.sum(tile_ids[:, None] >= tile_end[None, :], axis=1, dtype=jnp.int32), N_EXPERTS - 1)
    te_hot = te[:, None] == experts[None, :]
    tpick = lambda table: jnp.sum(jnp.where(te_hot, table[None, :], 0), axis=1)
    sorted_start = jnp.cumsum(counts) - counts
    left = tpick(counts) - (tile_ids * tm - tpick(row_start))
    tile_rows = jnp.where(tile_ids < used, jnp.clip(left, 0, tm), 0).astype(jnp.int32)
    last = jnp.sum((used - 1) >= tile_end, dtype=jnp.int32)
    tile_expert = jnp.where(tile_rows != 0, te, jnp.minimum(last, N_EXPERTS - 1))
    order = jnp.argsort(flat_e, stable=True).astype(jnp.int32)
    in_tile = jnp.arange(tm, dtype=jnp.int32)[None, :]
    valid = (in_tile < tile_rows[:, None]).reshape(n_tiles * tm)
    src = (tpick(sorted_start) + tile_ids * tm - tpick(row_start))[:, None] + in_tile
    assign = order[jnp.clip(src, 0, A - 1).reshape(n_tiles * tm)]
    row_token = jnp.where(valid, assign // TOP_K, 0)
    row_scale = jnp.where(valid, flat_p[assign], 0.0)
    return row_token, row_scale.reshape(n_tiles * tm, 1), pos, tile_expert, tile_rows


def _gather_kernel(tok_ref, x_hbm, o_ref, buf, sem):
    tm, D = o_ref.shape
    C = D // LANES
    pitch = _chunk_pitch(D)
    t = pl.program_id(0)

    def row_copy(tile, r):
        slot = tile % 2
        src = pl.multiple_of(tok_ref[tile * tm + r] * pitch, SUBLANES)
        dst = pl.multiple_of(r * pitch, SUBLANES)
        return pltpu.make_async_copy(x_hbm.at[pl.ds(src, C)], buf.at[slot, pl.ds(dst, C)], sem.at[slot])

    def start_tile(tile):
        def body(r8, c):
            for j in range(SUBLANES):
                row_copy(tile, r8 * SUBLANES + j).start(priority=j % 2)
            return c
        lax.fori_loop(0, tm // SUBLANES, body, 0)

    @pl.when(t == 0)
    def _():
        start_tile(0)

    @pl.when(t + 1 < pl.num_programs(0))
    def _():
        start_tile(t + 1)

    def wait(r, c):
        row_copy(t, r).wait()
        return c

    lax.fori_loop(0, tm, wait, 0, unroll=8)
    slot = t % 2
    for c in range(C):
        o_ref[:, c * LANES:(c + 1) * LANES] = buf[slot, pl.ds(c, tm, stride=pitch), :].astype(o_ref.dtype)


def _gather_rows(xc, row_token, D, tm=256):
    P = row_token.shape[0]
    grid_spec = pltpu.PrefetchScalarGridSpec(
        num_scalar_prefetch=1,
        grid=(P // tm,),
        in_specs=[pl.BlockSpec(memory_space=pl.ANY)],
        out_specs=pl.BlockSpec((tm, D), lambda t, tok: (t, 0)),
        scratch_shapes=[pltpu.VMEM((2, tm * _chunk_pitch(D), LANES), F32), pltpu.SemaphoreType.DMA((2,))],
    )
    return pl.pallas_call(
        _gather_kernel,
        grid_spec=grid_spec,
        out_shape=jax.ShapeDtypeStruct((P, D), BF16),
        compiler_params=_params("arbitrary"),
        name="gather_rows",
    )(row_token, xc)


def _combine_kernel(p0_ref, p1_ref, y_hbm, o_ref, buf0, buf1, sem):
    tm = buf0.shape[0]
    base = pl.program_id(0) * tm

    def copies(r):
        return (pltpu.make_async_copy(y_hbm.at[pl.ds(p0_ref[base + r], 1)], buf0.at[pl.ds(r, 1)], sem),
                pltpu.make_async_copy(y_hbm.at[pl.ds(p1_ref[base + r], 1)], buf1.at[pl.ds(r, 1)], sem))

    def start(r, c):
        for j, cp in enumerate(copies(r)):
            cp.start(priority=j)
        return c

    def wait(r, c):
        for cp in copies(r):
            cp.wait()
        return c

    lax.fori_loop(0, tm, start, 0)
    lax.fori_loop(0, tm, wait, 0)
    o_ref[...] = buf0[...] + buf1[...]


def _combine_rows(y, pos, tm=256):
    T = pos.shape[0]
    D = y.shape[1]
    grid_spec = pltpu.PrefetchScalarGridSpec(
        num_scalar_prefetch=2,
        grid=(T // tm,),
        in_specs=[pl.BlockSpec(memory_space=pl.ANY)],
        out_specs=pl.BlockSpec((tm, D), lambda t, p0, p1: (t, 0)),
        scratch_shapes=[pltpu.VMEM((tm, D), F32), pltpu.VMEM((tm, D), F32),
                        pltpu.SemaphoreType.DMA(())],
    )
    return pl.pallas_call(
        _combine_kernel,
        grid_spec=grid_spec,
        out_shape=jax.ShapeDtypeStruct((T, D), F32),
        compiler_params=_params("arbitrary"),
        name="combine_rows",
    )(pos[:, 0], pos[:, 1], y)


def _ple_kernel(h_ref, d_ref, p_ref, g_ref, wgd_ref, wgu_ref, wup_ref, *rest, n_norm):
    gn_refs = rest[:n_norm]
    o_ref = rest[n_norm]
    on_refs = rest[n_norm + 1:]
    h = h_ref[...] + d_ref[...]
    y = h * lax.rsqrt(jnp.mean(h * h, axis=-1, keepdims=True) + EPS) * g_ref[...]
    t = jnp.dot(y.astype(BF16), wgd_ref[...], preferred_element_type=F32)
    gate = _sigmoid(jnp.dot(t.astype(BF16), wgu_ref[...], preferred_element_type=F32))
    up = jnp.dot(p_ref[...].astype(BF16), wup_ref[...], preferred_element_type=F32)
    h = h + gate * up
    o_ref[...] = h
    if n_norm:
        y = h * lax.rsqrt(jnp.mean(h * h, axis=-1, keepdims=True) + EPS)
        for gn_ref, on_ref in zip(gn_refs, on_refs):
            on_ref[...] = (y * gn_ref[...]).astype(on_ref.dtype)


def _ple(h, delta, p_i, g_norm, w_up, w_gdown, w_gup, next_gains, tm=256):
    T, D = h.shape
    R = p_i.shape[1]
    row = pl.BlockSpec((tm, D), lambda i: (i, 0))
    gain = pl.BlockSpec((1, D), lambda i: (0, 0))
    n_norm = len(next_gains)
    outs = pl.pallas_call(
        functools.partial(_ple_kernel, n_norm=n_norm),
        grid=(T // tm,),
        in_specs=[row, row, pl.BlockSpec((tm, R), lambda i: (i, 0)), gain,
                  pl.BlockSpec((D, R), lambda i: (0, 0)),
                  pl.BlockSpec((R, D), lambda i: (0, 0)),
                  pl.BlockSpec((R, D), lambda i: (0, 0))] + [gain] * n_norm,
        out_specs=[row] * (1 + n_norm),
        out_shape=[jax.ShapeDtypeStruct((T, D), F32)] + [jax.ShapeDtypeStruct((T, D), BF16)] * n_norm,
        compiler_params=_params("parallel"),
        name="per_layer_embedding",
    )(h, delta, p_i, g_norm.reshape(1, D), w_gdown.astype(BF16), w_gup.astype(BF16),
      w_up.astype(BF16), *[g.reshape(1, D) for g in next_gains])
    return list(outs)


def kernel(x, p, norm_mix, norm_ffn, norm_ple, a_w_qkv, a_q_norm, a_k_norm, a_w_o, kv_norm, w_kv, b_w_q, b_w_o, ffn_w_gate, ffn_w_up, ffn_w_down, moe_w_router, moe_w_gate, moe_w_up, moe_w_down, ple_w_up, ple_w_gdown, ple_w_gup):
    B, S, D = x.shape
    T = B * S
    assert p.shape[0] == 2 and a_w_qkv.shape[0] == 1 and b_w_q.shape[0] == 1
    G = len(A_GROUPS)
    a_heads = a_w_o.shape[1] // HEAD_DIM
    b_heads = b_w_q.shape[2] // HEAD_DIM
    x2 = x.reshape(T, D)
    p2 = p.reshape(2, T, p.shape[-1])

    dils = [dil for _, dil in A_GROUPS]
    hns = _rmsnorm_dilated(x2, norm_mix[0], B, S, dils)
    os, sts = [], []
    for g, (span, dil) in enumerate(A_GROUPS):
        assert span // dil == ATTN_BLOCK
        qkv_g = _qkv_proj(hns[g].reshape(T, D), a_w_qkv[0], a_q_norm[0], a_k_norm[0], S, dil, g, G)
        o_g, st_g = _dilated_attention_group(qkv_g, B, S, dil, a_heads)
        os.append(o_g)
        sts.append(st_g)
    attn = _merge_groups(os, sts, dils, B, S, a_heads)
    h = _matmul(attn, a_w_o[0], F32, residual=x2, tn=1024)

    (hn,) = _rmsnorm(h, None, [norm_ffn[0]], [BF16])
    tm = 1024
    delta = _swiglu(hn, ffn_w_gate, ffn_w_up, ffn_w_down, jnp.zeros((T // tm,), jnp.int32),
                    jnp.full((T // tm,), tm, jnp.int32), None, tm=tm)
    h, hn_kv, hn_q = _ple(h, delta, p2[0], norm_ple[0], ple_w_up[0], ple_w_gdown[0], ple_w_gup[0],
                          [kv_norm, norm_mix[1]])

    kv = _matmul_heads(hn_kv, w_kv, B, S)
    q = _matmul_heads(hn_q, b_w_q[0], B, S)
    sb = _stick_breaking(q, kv, b_heads)
    h = _matmul(sb, b_w_o[0], F32, residual=h)

    xc, top_i, top_p = _norm_router(h, norm_ffn[1], moe_w_router[0])
    n_tiles = (T * TOP_K) // tm + N_EXPERTS
    row_token, row_scale, pos, tile_expert, tile_rows = _route_tables(top_i, top_p, tm, n_tiles)
    xs = _gather_rows(xc, row_token, D)
    ys = _swiglu(xs, moe_w_gate[0], moe_w_up[0], moe_w_down[0], tile_expert, tile_rows, row_scale, tm=tm)
    delta = _combine_rows(ys, pos)
    (h,) = _ple(h, delta, p2[1], norm_ple[1], ple_w_up[1], ple_w_gdown[1], ple_w_gup[1], [])
    return h.reshape(B, S, D)
```

```python
import functools
import math

import jax
import jax.numpy as jnp
from jax import lax
from jax.experimental import pallas as pl
from jax.experimental.pallas import tpu as pltpu

F32 = jnp.float32
BF16 = jnp.bfloat16

EPS = 1e-6
HEAD_DIM = 128
ROPE_THETA = 10000.0
A_GROUPS = ((128, 1), (512, 4), (2048, 16))
N_EXPERTS = 8
TOP_K = 2

LANES = 128
SUBLANES = 8
ATTN_BLOCK = 128
ATTN_HEAD_GROUP = 4
SB_BLOCK = 256
SB_UNROLL = 4
SB_DEAD = -104.0
ROW_STEP = 128
GATHER_TILE = 256
DOWN_CHUNK = 512
VMEM_LIMIT = 56 * 1024 * 1024

_NT = (((1,), (1,)), ((), ()))


def _params(*semantics, vmem_limit=VMEM_LIMIT):
    return pltpu.CompilerParams(dimension_semantics=semantics, vmem_limit_bytes=vmem_limit)


def _sigmoid(x):
    return 1.0 / (1.0 + jnp.exp(-x))


def _residue_perm(n_rows, dil, inverse):
    n = n_rows // dil
    assert dil & (dil - 1) == 0 and n & (n - 1) == 0
    out_row = lax.broadcasted_iota(jnp.int32, (n_rows, n_rows), 0)
    in_row = lax.broadcasted_iota(jnp.int32, (n_rows, n_rows), 1)
    if inverse:
        src = (out_row & (dil - 1)) * n + (out_row >> (dil.bit_length() - 1))
    else:
        src = (out_row & (n - 1)) * dil + (out_row >> (n.bit_length() - 1))
    return in_row == src


def _norm_kernel(*refs, has_delta, n_out):
    refs = list(refs)
    h_ref = refs.pop(0)
    d_ref = refs.pop(0) if has_delta else None
    g_refs = [refs.pop(0) for _ in range(n_out)]
    hsum_ref = refs.pop(0) if has_delta else None
    o_refs = refs
    h = h_ref[...]
    if has_delta:
        h = h + d_ref[...]
        hsum_ref[...] = h
    y = h * lax.rsqrt(jnp.mean(h * h, axis=-1, keepdims=True) + EPS)
    for g_ref, o_ref in zip(g_refs, o_refs):
        o_ref[...] = (y * g_ref[...]).astype(o_ref.dtype)


def _rmsnorm(h, delta, gains, out_dtypes, tm=256):
    T, D = h.shape
    has_delta = delta is not None
    row = pl.BlockSpec((tm, D), lambda i: (i, 0))
    gain = pl.BlockSpec((1, D), lambda i: (0, 0))
    ins = [h] + ([delta] if has_delta else []) + [g.reshape(1, D) for g in gains]
    in_specs = [row] * (2 if has_delta else 1) + [gain] * len(gains)
    out_shape = ([jax.ShapeDtypeStruct((T, D), F32)] if has_delta else []) + [
        jax.ShapeDtypeStruct((T, D), dt) for dt in out_dtypes]
    outs = pl.pallas_call(
        functools.partial(_norm_kernel, has_delta=has_delta, n_out=len(gains)),
        grid=(T // tm,),
        in_specs=in_specs,
        out_specs=[row] * len(out_shape),
        out_shape=out_shape,
        compiler_params=_params("parallel"),
        name="rmsnorm",
    )(*ins)
    return list(outs)


def _norm_dilated_kernel(h_ref, g_ref, *o_refs, dils):
    h = h_ref[...]
    tm = h.shape[0]
    y = (h * lax.rsqrt(jnp.mean(h * h, axis=-1, keepdims=True) + EPS) * g_ref[...]).astype(BF16)
    for o_ref, dil in zip(o_refs, dils):
        if dil == 1:
            o_ref[0, 0] = y
        else:
            perm = _residue_perm(tm, dil, inverse=False).astype(BF16)
            yp = jnp.dot(perm, y, preferred_element_type=F32).astype(BF16)
            n = tm // dil
            for r in range(dil):
                o_ref[0, r] = yp[r * n:(r + 1) * n]


def _rmsnorm_dilated(h, gain, B, S, dils, tm=256):
    T, D = h.shape
    bpb = S // tm
    return pl.pallas_call(
        functools.partial(_norm_dilated_kernel, dils=tuple(dils)),
        grid=(T // tm,),
        in_specs=[pl.BlockSpec((tm, D), lambda i: (i, 0)), pl.BlockSpec((1, D), lambda i: (0, 0))],
        out_specs=[pl.BlockSpec((1, d, tm // d, D), lambda i: (i // bpb, 0, i % bpb, 0)) for d in dils],
        out_shape=[jax.ShapeDtypeStruct((B, d, S // d, D), BF16) for d in dils],
        compiler_params=_params("parallel"),
        name="rmsnorm_dilated",
    )(h, gain.reshape(1, D))


def _mm_kernel(x_ref, w_ref, *rest, has_res):
    acc = jnp.dot(x_ref[...], w_ref[...].astype(BF16), preferred_element_type=F32)
    if has_res:
        r_ref, o_ref = rest
        acc = r_ref[...] + acc
    else:
        (o_ref,) = rest
    o_ref[...] = acc.astype(o_ref.dtype)


def _matmul(x, w, out_dtype, residual=None, tm=1024, tn=512):
    M, K = x.shape
    N = w.shape[1]
    has_res = residual is not None
    in_specs = [pl.BlockSpec((tm, K), lambda i, j: (i, 0)),
                pl.BlockSpec((K, tn), lambda i, j: (0, j))]
    ins = [x, w]
    if has_res:
        in_specs.append(pl.BlockSpec((tm, tn), lambda i, j: (i, j)))
        ins.append(residual)
    return pl.pallas_call(
        functools.partial(_mm_kernel, has_res=has_res),
        grid=(M // tm, N // tn),
        in_specs=in_specs,
        out_specs=pl.BlockSpec((tm, tn), lambda i, j: (i, j)),
        out_shape=jax.ShapeDtypeStruct((M, N), out_dtype),
        compiler_params=_params("parallel", "arbitrary"),
        name="matmul",
    )(*ins)


def _mm_heads_kernel(x_ref, w_ref, o_ref):
    acc = jnp.dot(x_ref[...], w_ref[...].astype(BF16), preferred_element_type=F32)
    for hh in range(o_ref.shape[0]):
        o_ref[hh] = acc[:, hh * HEAD_DIM:(hh + 1) * HEAD_DIM].astype(o_ref.dtype)


def _matmul_heads(x, w, B, S, tm=1024, tn=512):
    M, K = x.shape
    N = w.shape[1]
    bpb = S // tm
    hpt = tn // HEAD_DIM
    return pl.pallas_call(
        _mm_heads_kernel,
        grid=(M // tm, N // tn),
        in_specs=[pl.BlockSpec((tm, K), lambda i, j: (i, 0)),
                  pl.BlockSpec((K, tn), lambda i, j: (0, j))],
        out_specs=pl.BlockSpec((None, hpt, tm, HEAD_DIM), lambda i, j: (i // bpb, j, i % bpb, 0)),
        out_shape=jax.ShapeDtypeStruct((B, N // HEAD_DIM, S, HEAD_DIM), BF16),
        compiler_params=_params("parallel", "arbitrary"),
        name="matmul_heads",
    )(x, w)


def _qkv_kernel(x_ref, w_ref, cos_ref, sin_ref, gq_ref, gk_ref, o_ref, acc_a, acc_b, *,
                q_tiles, qk_tiles, n_tiles):
    j = pl.program_id(1)

    def matmul(acc_ref):
        acc_ref[...] = jnp.dot(x_ref[...], w_ref[...].astype(BF16), preferred_element_type=F32)

    def finish(acc_ref):
        jp = j - 1
        is_qk = jp < qk_tiles
        gain = jnp.where(jp < q_tiles, gq_ref[...], gk_ref[...])
        cos = cos_ref[...]
        sin = sin_ref[...]
        for hh in range(acc_ref.shape[1] // HEAD_DIM):
            sl = slice(hh * HEAD_DIM, (hh + 1) * HEAD_DIM)
            blk = acc_ref[:, sl]
            y = blk * lax.rsqrt(jnp.mean(blk * blk, axis=-1, keepdims=True) + EPS) * gain
            y = y * cos + pltpu.roll(y, HEAD_DIM // 2, 1) * sin
            o_ref[:, sl] = jnp.where(is_qk, y, blk).astype(o_ref.dtype)

    even = j % 2 == 0

    @pl.when(j == 0)
    def _():
        matmul(acc_a)

    @pl.when((j > 0) & (j < n_tiles) & even)
    def _():
        matmul(acc_a)
        finish(acc_b)

    @pl.when((j < n_tiles) & jnp.logical_not(even))
    def _():
        matmul(acc_b)
        finish(acc_a)

    @pl.when(j == n_tiles)
    def _():
        finish(acc_b if n_tiles % 2 == 0 else acc_a)


def _rope_tables(S, dil):
    half = HEAD_DIM // 2
    inv = ROPE_THETA ** (-jnp.arange(half, dtype=F32) / half)
    pos = jnp.arange(S, dtype=F32).reshape(S // dil, dil).T.reshape(S)
    ang = pos[:, None] * inv[None, :]
    cos, sin = jnp.cos(ang), jnp.sin(ang)
    return jnp.concatenate([cos, cos], axis=-1), jnp.concatenate([-sin, sin], axis=-1)


def _qkv_proj(x, w, g_q, g_k, S, dil, group, n_groups, tm=1024, tn=512):
    M, K = x.shape
    hw = w.shape[1] // (3 * n_groups)
    tpg = hw // tn
    cos, sin = _rope_tables(S, dil)
    pos_blocks = S // tm
    n_tiles = 3 * tpg

    def w_col(i, j):
        jc = jnp.minimum(j, n_tiles - 1)
        return (0, ((jc // tpg) * n_groups + group) * tpg + jc % tpg)

    return pl.pallas_call(
        functools.partial(_qkv_kernel, q_tiles=tpg, qk_tiles=2 * tpg, n_tiles=n_tiles),
        grid=(M // tm, n_tiles + 1),
        in_specs=[pl.BlockSpec((tm, K), lambda i, j: (i, 0)),
                  pl.BlockSpec((K, tn), w_col),
                  pl.BlockSpec((tm, HEAD_DIM), lambda i, j: (i % pos_blocks, 0)),
                  pl.BlockSpec((tm, HEAD_DIM), lambda i, j: (i % pos_blocks, 0)),
                  pl.BlockSpec((1, HEAD_DIM), lambda i, j: (0, 0)),
                  pl.BlockSpec((1, HEAD_DIM), lambda i, j: (0, 0))],
        out_specs=pl.BlockSpec((tm, tn), lambda i, j: (i, jnp.maximum(j - 1, 0))),
        out_shape=jax.ShapeDtypeStruct((M, 3 * hw), BF16),
        scratch_shapes=[pltpu.VMEM((tm, tn), F32), pltpu.VMEM((tm, tn), F32)],
        compiler_params=_params("parallel", "arbitrary"),
        name=f"qkv_proj_d{dil}",
    )(x, w, cos, sin, g_q.reshape(1, HEAD_DIM), g_k.reshape(1, HEAD_DIM))


def _dil_attn_kernel(q_ref, kp_ref, kc_ref, vp_ref, vc_ref, o_ref, st_ref, *, n_heads, scale):
    n = pl.program_id(2)
    blk = q_ref.shape[0]
    i = lax.broadcasted_iota(jnp.int32, (blk, blk), 0)
    j = lax.broadcasted_iota(jnp.int32, (blk, blk), 1)
    mask_p = (j >= i) & (n > 0)
    mask_c = j <= i
    lane = lax.broadcasted_iota(jnp.int32, (blk, LANES), 1)
    stats = jnp.zeros((blk, LANES), F32)
    for h0 in range(0, n_heads, ATTN_HEAD_GROUP):
        sls = [slice(h * HEAD_DIM, (h + 1) * HEAD_DIM) for h in range(h0, h0 + ATTN_HEAD_GROUP)]
        s_ps = [lax.dot_general(q_ref[:, sl], kp_ref[:, sl], _NT, preferred_element_type=F32) for sl in sls]
        s_cs = [lax.dot_general(q_ref[:, sl], kc_ref[:, sl], _NT, preferred_element_type=F32) for sl in sls]
        p_ps, p_cs, ls = [], [], []
        for u, (s_p, s_c) in enumerate(zip(s_ps, s_cs)):
            s_p = jnp.where(mask_p, s_p * scale, -jnp.inf)
            s_c = jnp.where(mask_c, s_c * scale, -jnp.inf)
            m = jnp.maximum(jnp.max(s_p, axis=-1, keepdims=True), jnp.max(s_c, axis=-1, keepdims=True))
            p_p = jnp.exp(s_p - m)
            p_c = jnp.exp(s_c - m)
            l = jnp.sum(p_p, axis=-1, keepdims=True) + jnp.sum(p_c, axis=-1, keepdims=True)
            p_ps.append(p_p.astype(BF16))
            p_cs.append(p_c.astype(BF16))
            ls.append(l)
            stats = jnp.where(lane == h0 + u, m + jnp.log(l), stats)
        for sl, p_p, p_c, l in zip(sls, p_ps, p_cs, ls):
            o = (jnp.dot(p_p, vp_ref[:, sl], preferred_element_type=F32)
                 + jnp.dot(p_c, vc_ref[:, sl], preferred_element_type=F32))
            o_ref[:, sl] = (o / l).astype(o_ref.dtype)
    st_ref[...] = stats


def _dilated_attention_group(qkv_g, B, S, dil, n_heads):
    hw = n_heads * HEAD_DIM
    L = S // dil
    nb = L // ATTN_BLOCK
    a = qkv_g.reshape(B, dil, L, 3 * hw)
    blk = (None, None, ATTN_BLOCK, hw)
    cur = lambda c: (lambda b, r, n: (b, r, n, c))
    prev = lambda c: (lambda b, r, n: (b, r, jnp.maximum(n - 1, 0), c))
    return pl.pallas_call(
        functools.partial(_dil_attn_kernel, n_heads=n_heads, scale=1.0 / math.sqrt(HEAD_DIM)),
        grid=(B, dil, nb),
        in_specs=[pl.BlockSpec(blk, cur(0)),
                  pl.BlockSpec(blk, prev(1)), pl.BlockSpec(blk, cur(1)),
                  pl.BlockSpec(blk, prev(2)), pl.BlockSpec(blk, cur(2))],
        out_specs=[pl.BlockSpec(blk, lambda b, r, n: (b, r, n, 0)),
                   pl.BlockSpec((None, None, ATTN_BLOCK, LANES), lambda b, r, n: (b, r, n, 0))],
        out_shape=[jax.ShapeDtypeStruct((B, dil, L, hw), BF16),
                   jax.ShapeDtypeStruct((B, dil, L, LANES), F32)],
        compiler_params=_params("parallel", "parallel", "arbitrary"),
        name=f"dilated_attn_d{dil}",
    )(a, a, a, a, a)


def _merge_kernel(*refs, dils, n_heads):
    G = len(dils)
    o_refs, s_refs, out_ref = refs[:G], refs[G:2 * G], refs[2 * G]
    tm, hw = out_ref.shape
    o, s = [], []
    for o_ref, s_ref, dil in zip(o_refs, s_refs, dils):
        o_g = o_ref[...].reshape(tm, hw)
        s_g = s_ref[...].reshape(tm, LANES)
        if dil == 1:
            o_g = o_g.astype(F32)
        else:
            pinv = _residue_perm(tm, dil, inverse=True)
            o_g = jnp.dot(pinv.astype(BF16), o_g, preferred_element_type=F32)
            s_g = jnp.dot(pinv.astype(F32), s_g, preferred_element_type=F32,
                          precision=lax.Precision.HIGHEST)
        o.append(o_g)
        s.append(s_g)
    m = functools.reduce(jnp.maximum, s)
    e = [jnp.exp(x - m) for x in s]
    den = functools.reduce(lambda a, b: a + b, e)
    w = [x / den for x in e]
    for h in range(n_heads):
        sl = slice(h * HEAD_DIM, (h + 1) * HEAD_DIM)
        acc = w[0][:, h:h + 1] * o[0][:, sl]
        for g in range(1, G):
            acc = acc + w[g][:, h:h + 1] * o[g][:, sl]
        out_ref[:, sl] = acc.astype(out_ref.dtype)


def _merge_groups(os, sts, dils, B, S, n_heads, tm=256):
    hw = n_heads * HEAD_DIM
    bpb = S // tm
    spec = lambda d, w: pl.BlockSpec((None, d, tm // d, w), lambda i: (i // bpb, 0, i % bpb, 0))
    return pl.pallas_call(
        functools.partial(_merge_kernel, dils=tuple(dils), n_heads=n_heads),
        grid=(B * bpb,),
        in_specs=[spec(d, hw) for d in dils] + [spec(d, LANES) for d in dils],
        out_specs=pl.BlockSpec((tm, hw), lambda i: (i, 0)),
        out_shape=jax.ShapeDtypeStruct((B * S, hw), BF16),
        compiler_params=_params("parallel"),
        name="merge_groups",
    )(*os, *sts)


def _sb_kernel(q_ref, k_ref, v_ref, o_ref, *, scale):
    tq = SB_BLOCK
    row = lax.broadcasted_iota(jnp.int32, (tq, tq), 0)
    col = lax.broadcasted_iota(jnp.int32, (tq, tq), 1)
    strict = col < row
    tri = (row > col).astype(BF16)

    def tiles(qs, jobs, carries):
        starts = [pl.multiple_of(kb * tq, tq) for _, kb, _ in jobs]
        zs = [lax.dot_general(qs[j], k_ref[pl.ds(s, tq), :], _NT, preferred_element_type=F32) * scale
              for (j, _, _), s in zip(jobs, starts)]
        lks, lss = [], []
        for (_, _, diag), z in zip(jobs, zs):
            lk = -(jnp.maximum(z, 0.0) + jnp.log(1.0 + jnp.exp(-jnp.abs(z))))
            lss.append(z + lk)
            lks.append(jnp.where(strict, lk, 0.0) if diag else lk)
        sufs = [jnp.dot(lk.astype(BF16), tri, preferred_element_type=F32) for lk in lks]
        accs = [c[0] for c in carries]
        runs = [c[1] for c in carries]
        ps = []
        for u, (j, _, diag) in enumerate(jobs):
            a = jnp.exp(lss[u] + (sufs[u] + runs[j]))
            ps.append((jnp.where(strict, a, 0.0) if diag else a).astype(BF16))
            runs[j] = runs[j] + jnp.sum(lks[u], axis=1, keepdims=True)
        for u, (j, _, _) in enumerate(jobs):
            accs[j] = accs[j] + jnp.dot(ps[u], v_ref[pl.ds(starts[u], tq), :], preferred_element_type=F32)
        return tuple(zip(accs, runs))

    alive = lambda r: (jnp.max(r) > SB_DEAD).astype(jnp.int32)

    def finish(q, qi, carry):
        rest = jnp.maximum(qi - 1, 0)
        rem = rest % SB_UNROLL

        def walk(n, first, count, state):
            def body(c):
                t, acc, run, _ = c
                jobs = [(0, first - t * n - u, False) for u in range(n)]
                ((acc, run),) = tiles([q], jobs, [(acc, run)])
                return t + 1, acc, run, alive(run)
            return lax.while_loop(lambda c: (c[0] < count) & (c[3] != 0), body, (0,) + state)[1:]

        state = carry + (alive(carry[1]),)
        state = walk(1, rest - 1, rem, state)
        state = walk(SB_UNROLL, rest - 1 - rem, rest // SB_UNROLL, state)
        return state[0]

    def q_pair(p, _):
        qa, qb = 2 * p, 2 * p + 1
        rows = [pl.ds(pl.multiple_of(qi * tq, tq), tq) for qi in (qa, qb)]
        qs = [q_ref[r, :] for r in rows]
        zero = (jnp.zeros((tq, HEAD_DIM), F32), jnp.zeros((tq, 1), F32))
        first_pair = [(0, qa, True), (1, qb, True), (1, qa, False)]
        later_pair = [(0, qa, True), (0, qa - 1, False), (1, qb, True), (1, qa, False)]
        carries = lax.cond(p == 0, functools.partial(tiles, qs, first_pair),
                           functools.partial(tiles, qs, later_pair), (zero, zero))
        for q, qi, r, carry in zip(qs, (qa, qb), rows, carries):
            o_ref[r, :] = finish(q, qi, carry).astype(o_ref.dtype)
        return 0

    lax.fori_loop(0, q_ref.shape[0] // (2 * tq), q_pair, 0)


def _stick_breaking(q, kv, n_heads):
    B, _, S, _ = q.shape
    head = lambda off: pl.BlockSpec((None, None, S, HEAD_DIM), lambda b, h: (b, off + h, 0, 0))
    out = pl.pallas_call(
        functools.partial(_sb_kernel, scale=1.0 / math.sqrt(HEAD_DIM)),
        grid=(B, n_heads),
        in_specs=[head(0), head(0), head(n_heads)],
        out_specs=pl.BlockSpec((None, S, HEAD_DIM), lambda b, h: (b, 0, h)),
        out_shape=jax.ShapeDtypeStruct((B, S, n_heads * HEAD_DIM), BF16),
        compiler_params=_params("parallel", "arbitrary"),
        name="stick_breaking",
    )(q, kv, kv)
    return out.reshape(B * S, n_heads * HEAD_DIM)


def _swiglu_kernel(te_ref, tv_ref, x_ref, wg_ref, wu_ref, wd_ref, sc_ref, o_ref, *, use_scale):
    t = pl.program_id(0)
    f = pl.program_id(1)
    tm = x_ref.shape[0]
    steps = (tv_ref[t] + ROW_STEP - 1) // ROW_STEP

    def body(m):
        x = x_ref[:m, :]
        g = jnp.dot(x, wg_ref[...].astype(BF16), preferred_element_type=F32)
        u = jnp.dot(x, wu_ref[...].astype(BF16), preferred_element_type=F32)
        mid = (g * _sigmoid(g) * u).astype(BF16)

        @pl.when(f == 0)
        def _():
            o_ref[...] = jnp.zeros_like(o_ref)

        for c in range(0, o_ref.shape[1], DOWN_CHUNK):
            sl = slice(c, c + DOWN_CHUNK)
            o_ref[:m, sl] += jnp.dot(mid, wd_ref[:, sl].astype(BF16), preferred_element_type=F32)

        if use_scale:
            @pl.when(f == pl.num_programs(1) - 1)
            def _():
                o_ref[:m, :] *= sc_ref[:m, :]

    for k in range(1, tm // ROW_STEP + 1):
        pl.when(steps == k)(functools.partial(body, k * ROW_STEP))

    @pl.when((steps == 0) & (f == 0))
    def _():
        o_ref[...] = jnp.zeros_like(o_ref)


def _swiglu(x, w_gate, w_up, w_down, tile_expert, tile_rows, row_scale, tm=1024, tf=256):
    P, D = x.shape
    E, _, F = w_gate.shape
    nf = F // tf
    use_scale = row_scale is not None
    if not use_scale:
        row_scale = jnp.ones((P, 1), F32)
    fidx = lambda t, f, tv: jnp.where(tv[t] != 0, f, nf - 1)
    grid_spec = pltpu.PrefetchScalarGridSpec(
        num_scalar_prefetch=2,
        grid=(P // tm, nf),
        in_specs=[pl.BlockSpec((tm, D), lambda t, f, te, tv: (t, 0), pipeline_mode=pl.Buffered(1)),
                  pl.BlockSpec((None, D, tf), lambda t, f, te, tv: (te[t], 0, fidx(t, f, tv))),
                  pl.BlockSpec((None, D, tf), lambda t, f, te, tv: (te[t], 0, fidx(t, f, tv))),
                  pl.BlockSpec((None, tf, D), lambda t, f, te, tv: (te[t], fidx(t, f, tv), 0)),
                  pl.BlockSpec((tm, 1), lambda t, f, te, tv: (t, 0))],
        out_specs=pl.BlockSpec((tm, D), lambda t, f, te, tv: (t, 0), pipeline_mode=pl.Buffered(1)),
    )
    return pl.pallas_call(
        functools.partial(_swiglu_kernel, use_scale=use_scale),
        grid_spec=grid_spec,
        out_shape=jax.ShapeDtypeStruct((P, D), F32),
        compiler_params=_params("parallel", "arbitrary"),
        name="swiglu",
    )(tile_expert, tile_rows, x, w_gate, w_up, w_down, row_scale)


def _router_kernel(h_ref, g_ref, w_ref, xc_ref, idx_ref, p_ref, *, n_experts):
    h = h_ref[...]
    tm, D = h.shape
    x = h * lax.rsqrt(jnp.mean(h * h, axis=-1, keepdims=True) + EPS) * g_ref[...]
    C = D // LANES
    pitch = _chunk_pitch(D)
    for c in range(C):
        xc_ref[pl.ds(c, tm, stride=pitch), :] = x[:, c * LANES:(c + 1) * LANES]
    for c in range(C, pitch):
        xc_ref[pl.ds(c, tm, stride=pitch), :] = jnp.zeros((tm, LANES), F32)
    logits = jnp.dot(x, w_ref[...], preferred_element_type=F32, precision=lax.Precision.HIGHEST)
    lane = lax.broadcasted_iota(jnp.int32, logits.shape, 1)
    logits = jnp.where(lane < n_experts, logits, -jnp.inf)
    m1 = jnp.max(logits, axis=-1, keepdims=True)
    i1 = jnp.min(jnp.where(logits == m1, lane, LANES), axis=-1, keepdims=True)
    rest = jnp.where(lane == i1, -jnp.inf, logits)
    m2 = jnp.max(rest, axis=-1, keepdims=True)
    i2 = jnp.min(jnp.where(rest == m2, lane, LANES), axis=-1, keepdims=True)
    e = jnp.exp(m2 - m1)
    p1 = 1.0 / (1.0 + e)
    p2 = e / (1.0 + e)
    idx_ref[...] = jnp.where(lane == 0, i1, jnp.where(lane == 1, i2, 0))
    p_ref[...] = jnp.where(lane == 0, p1, jnp.where(lane == 1, p2, 0.0))


def _chunk_pitch(D):
    return D // LANES + SUBLANES


def _norm_router(h, gain, w_router, tm=256):
    T, D = h.shape
    E = w_router.shape[1]
    C = _chunk_pitch(D)
    w = jnp.pad(w_router, ((0, 0), (0, LANES - E)))
    xc, idx, prob = pl.pallas_call(
        functools.partial(_router_kernel, n_experts=E),
        grid=(T // tm,),
        in_specs=[pl.BlockSpec((tm, D), lambda i: (i, 0)),
                  pl.BlockSpec((1, D), lambda i: (0, 0)),
                  pl.BlockSpec((D, LANES), lambda i: (0, 0))],
        out_specs=[pl.BlockSpec((tm * C, LANES), lambda i: (i, 0)),
                   pl.BlockSpec((tm, LANES), lambda i: (i, 0)),
                   pl.BlockSpec((tm, LANES), lambda i: (i, 0))],
        out_shape=[jax.ShapeDtypeStruct((T * C, LANES), F32),
                   jax.ShapeDtypeStruct((T, LANES), jnp.int32),
                   jax.ShapeDtypeStruct((T, LANES), F32)],
        compiler_params=_params("parallel"),
        name="norm_router",
    )(h, gain.reshape(1, D), w)
    return xc, idx[:, :TOP_K], prob[:, :TOP_K]


def _route_tables(top_i, top_p, tm, n_tiles):
    T = top_i.shape[0]
    A = T * TOP_K
    flat_e = top_i.reshape(A)
    flat_p = top_p.reshape(A)
    experts = jnp.arange(N_EXPERTS, dtype=jnp.int32)
    onehot = flat_e[:, None] == experts[None, :]
    running = jnp.cumsum(onehot.astype(jnp.int32), axis=0)
    counts = running[-1]
    pick = lambda table: jnp.sum(jnp.where(onehot, table, 0), axis=1)
    rank = pick(running) - 1
    tiles_per = (counts + tm - 1) // tm
    per = (counts + jnp.maximum(tiles_per, 1) - 1) // jnp.maximum(tiles_per, 1)
    tile_end = jnp.cumsum(tiles_per)
    tile_start = tile_end - tiles_per
    per_a = jnp.maximum(pick(per[None, :]), 1)
    pos = ((pick(tile_start[None, :]) + rank // per_a) * tm + rank % per_a).reshape(T, TOP_K)
    tile_ids = jnp.arange(n_tiles, dtype=jnp.int32)
    used = tile_end[-1]
    te = jnp.minimum(jnp.sum(tile_ids[:, None] >= tile_end[None, :], axis=1, dtype=jnp.int32), N_EXPERTS - 1)
    te_hot = te[:, None] == experts[None, :]
    tpick = lambda table: jnp.sum(jnp.where(te_hot, table[None, :], 0), axis=1)
    sorted_start = jnp.cumsum(counts) - counts
    before = (tile_ids - tpick(tile_start)) * tpick(per)
    tile_rows = jnp.where(tile_ids < used, jnp.clip(tpick(counts) - before, 0, tpick(per)), 0).astype(jnp.int32)
    last = jnp.sum((used - 1) >= tile_end, dtype=jnp.int32)
    tile_expert = jnp.where(tile_rows != 0, te, jnp.minimum(last, N_EXPERTS - 1))
    order = jnp.argsort(flat_e, stable=True).astype(jnp.int32)
    in_tile = jnp.arange(tm, dtype=jnp.int32)[None, :]
    valid = (in_tile < tile_rows[:, None]).reshape(n_tiles * tm)
    src = (tpick(sorted_start) + before)[:, None] + in_tile
    assign = order[jnp.clip(src, 0, A - 1).reshape(n_tiles * tm)]
    row_token = jnp.where(valid, assign // TOP_K, 0)
    row_scale = jnp.where(valid, flat_p[assign], 0.0)
    return row_token, row_scale.reshape(n_tiles * tm, 1), pos, tile_expert, tile_rows


def _gather_kernel(tok_ref, need_ref, x_hbm, o_ref, buf, sem):
    tm, D = o_ref.shape
    C = D // LANES
    pitch = _chunk_pitch(D)
    t = pl.program_id(0)

    def row_copy(tile, r):
        slot = tile % 2
        src = pl.multiple_of(tok_ref[tile * tm + r] * pitch, SUBLANES)
        dst = pl.multiple_of(r * pitch, SUBLANES)
        return pltpu.make_async_copy(x_hbm.at[pl.ds(src, C)], buf.at[slot, pl.ds(dst, C)], sem.at[slot])

    def start_tile(tile):
        def body(r8, c):
            for j in range(SUBLANES):
                row_copy(tile, r8 * SUBLANES + j).start(priority=j % 2)
            return c
        lax.fori_loop(0, tm // SUBLANES, body, 0)

    @pl.when((t == 0) & (need_ref[0] != 0))
    def _():
        start_tile(0)

    nxt = jnp.minimum(t + 1, pl.num_programs(0) - 1)

    @pl.when((t + 1 < pl.num_programs(0)) & (need_ref[nxt] != 0))
    def _():
        start_tile(t + 1)

    @pl.when(need_ref[t] != 0)
    def _():
        def wait(r, c):
            row_copy(t, r).wait()
            return c

        lax.fori_loop(0, tm, wait, 0, unroll=8)
        slot = t % 2
        for c in range(C):
            o_ref[:, c * LANES:(c + 1) * LANES] = buf[slot, pl.ds(c, tm, stride=pitch), :].astype(o_ref.dtype)

    @pl.when(need_ref[t] == 0)
    def _():
        o_ref[...] = jnp.zeros_like(o_ref)


def _gather_rows(xc, row_token, need, D, tm):
    P = row_token.shape[0]
    grid_spec = pltpu.PrefetchScalarGridSpec(
        num_scalar_prefetch=2,
        grid=(P // tm,),
        in_specs=[pl.BlockSpec(memory_space=pl.ANY)],
        out_specs=pl.BlockSpec((tm, D), lambda t, tok, need: (t, 0)),
        scratch_shapes=[pltpu.VMEM((2, tm * _chunk_pitch(D), LANES), F32), pltpu.SemaphoreType.DMA((2,))],
    )
    return pl.pallas_call(
        _gather_kernel,
        grid_spec=grid_spec,
        out_shape=jax.ShapeDtypeStruct((P, D), BF16),
        compiler_params=_params("arbitrary"),
        name="gather_rows",
    )(row_token, need, xc)


def _combine_kernel(p0_ref, p1_ref, y_hbm, o_ref, buf0, buf1, sem):
    tm = buf0.shape[0]
    base = pl.program_id(0) * tm

    def copies(r):
        return (pltpu.make_async_copy(y_hbm.at[pl.ds(p0_ref[base + r], 1)], buf0.at[pl.ds(r, 1)], sem),
                pltpu.make_async_copy(y_hbm.at[pl.ds(p1_ref[base + r], 1)], buf1.at[pl.ds(r, 1)], sem))

    def start(r, c):
        for j, cp in enumerate(copies(r)):
            cp.start(priority=j)
        return c

    def wait(r, c):
        for cp in copies(r):
            cp.wait()
        return c

    lax.fori_loop(0, tm, start, 0)
    lax.fori_loop(0, tm, wait, 0)
    o_ref[...] = buf0[...] + buf1[...]


def _combine_rows(y, pos, tm=256):
    T = pos.shape[0]
    D = y.shape[1]
    grid_spec = pltpu.PrefetchScalarGridSpec(
        num_scalar_prefetch=2,
        grid=(T // tm,),
        in_specs=[pl.BlockSpec(memory_space=pl.ANY)],
        out_specs=pl.BlockSpec((tm, D), lambda t, p0, p1: (t, 0)),
        scratch_shapes=[pltpu.VMEM((tm, D), F32), pltpu.VMEM((tm, D), F32),
                        pltpu.SemaphoreType.DMA(())],
    )
    return pl.pallas_call(
        _combine_kernel,
        grid_spec=grid_spec,
        out_shape=jax.ShapeDtypeStruct((T, D), F32),
        compiler_params=_params("arbitrary"),
        name="combine_rows",
    )(pos[:, 0], pos[:, 1], y)


def _ple_kernel(h_ref, d_ref, p_ref, g_ref, wgd_ref, wgu_ref, wup_ref, *rest, n_norm):
    gn_refs = rest[:n_norm]
    o_ref = rest[n_norm]
    on_refs = rest[n_norm + 1:]
    h = h_ref[...] + d_ref[...]
    y = h * lax.rsqrt(jnp.mean(h * h, axis=-1, keepdims=True) + EPS) * g_ref[...]
    t = jnp.dot(y.astype(BF16), wgd_ref[...], preferred_element_type=F32)
    gate = _sigmoid(jnp.dot(t.astype(BF16), wgu_ref[...], preferred_element_type=F32))
    up = jnp.dot(p_ref[...].astype(BF16), wup_ref[...], preferred_element_type=F32)
    h = h + gate * up
    o_ref[...] = h
    if n_norm:
        y = h * lax.rsqrt(jnp.mean(h * h, axis=-1, keepdims=True) + EPS)
        for gn_ref, on_ref in zip(gn_refs, on_refs):
            on_ref[...] = (y * gn_ref[...]).astype(on_ref.dtype)


def _ple(h, delta, p_i, g_norm, w_up, w_gdown, w_gup, next_gains, tm=256):
    T, D = h.shape
    R = p_i.shape[1]
    row = pl.BlockSpec((tm, D), lambda i: (i, 0))
    gain = pl.BlockSpec((1, D), lambda i: (0, 0))
    n_norm = len(next_gains)
    outs = pl.pallas_call(
        functools.partial(_ple_kernel, n_norm=n_norm),
        grid=(T // tm,),
        in_specs=[row, row, pl.BlockSpec((tm, R), lambda i: (i, 0)), gain,
                  pl.BlockSpec((D, R), lambda i: (0, 0)),
                  pl.BlockSpec((R, D), lambda i: (0, 0)),
                  pl.BlockSpec((R, D), lambda i: (0, 0))] + [gain] * n_norm,
        out_specs=[row] * (1 + n_norm),
        out_shape=[jax.ShapeDtypeStruct((T, D), F32)] + [jax.ShapeDtypeStruct((T, D), BF16)] * n_norm,
        compiler_params=_params("parallel"),
        name="per_layer_embedding",
    )(h, delta, p_i, g_norm.reshape(1, D), w_gdown.astype(BF16), w_gup.astype(BF16),
      w_up.astype(BF16), *[g.reshape(1, D) for g in next_gains])
    return list(outs)


def kernel(x, p, norm_mix, norm_ffn, norm_ple, a_w_qkv, a_q_norm, a_k_norm, a_w_o, kv_norm, w_kv, b_w_q, b_w_o, ffn_w_gate, ffn_w_up, ffn_w_down, moe_w_router, moe_w_gate, moe_w_up, moe_w_down, ple_w_up, ple_w_gdown, ple_w_gup):
    B, S, D = x.shape
    T = B * S
    assert p.shape[0] == 2 and a_w_qkv.shape[0] == 1 and b_w_q.shape[0] == 1
    G = len(A_GROUPS)
    a_heads = a_w_o.shape[1] // HEAD_DIM
    b_heads = b_w_q.shape[2] // HEAD_DIM
    x2 = x.reshape(T, D)
    p2 = p.reshape(2, T, p.shape[-1])

    dils = [dil for _, dil in A_GROUPS]
    hns = _rmsnorm_dilated(x2, norm_mix[0], B, S, dils)
    os, sts = [], []
    for g, (span, dil) in enumerate(A_GROUPS):
        assert span // dil == ATTN_BLOCK
        qkv_g = _qkv_proj(hns[g].reshape(T, D), a_w_qkv[0], a_q_norm[0], a_k_norm[0], S, dil, g, G)
        o_g, st_g = _dilated_attention_group(qkv_g, B, S, dil, a_heads)
        os.append(o_g)
        sts.append(st_g)
    attn = _merge_groups(os, sts, dils, B, S, a_heads)
    h = _matmul(attn, a_w_o[0], F32, residual=x2, tn=1024)

    (hn,) = _rmsnorm(h, None, [norm_ffn[0]], [BF16])
    tm = 1024
    delta = _swiglu(hn, ffn_w_gate, ffn_w_up, ffn_w_down, jnp.zeros((T // tm,), jnp.int32),
                    jnp.full((T // tm,), tm, jnp.int32), None, tm=tm)
    h, hn_kv, hn_q = _ple(h, delta, p2[0], norm_ple[0], ple_w_up[0], ple_w_gdown[0], ple_w_gup[0],
                          [kv_norm, norm_mix[1]])

    kv = _matmul_heads(hn_kv, w_kv, B, S)
    q = _matmul_heads(hn_q, b_w_q[0], B, S)
    sb = _stick_breaking(q, kv, b_heads)
    h = _matmul(sb, b_w_o[0], F32, residual=h)

    xc, top_i, top_p = _norm_router(h, norm_ffn[1], moe_w_router[0])
    n_tiles = (T * TOP_K) // tm + N_EXPERTS
    row_token, row_scale, pos, tile_expert, tile_rows = _route_tables(top_i, top_p, tm, n_tiles)
    assert GATHER_TILE % ROW_STEP == 0
    first_row = jnp.arange(tm // GATHER_TILE, dtype=jnp.int32)[None, :] * GATHER_TILE
    need = (first_row < tile_rows[:, None]).astype(jnp.int32).reshape(-1)
    xs = _gather_rows(xc, row_token, need, D, GATHER_TILE)
    ys = _swiglu(xs, moe_w_gate[0], moe_w_up[0], moe_w_down[0], tile_expert, tile_rows, row_scale, tm=tm)
    delta = _combine_rows(ys, pos)
    (h,) = _ple(h, delta, p2[1], norm_ple[1], ple_w_up[1], ple_w_gdown[1], ple_w_gup[1], [])
    return h.reshape(B, S, D)
```

```python
import functools
import math

import jax
import jax.numpy as jnp
from jax import lax
from jax.experimental import pallas as pl
from jax.experimental.pallas import tpu as pltpu

F32 = jnp.float32
BF16 = jnp.bfloat16

EPS = 1e-6
HEAD_DIM = 128
ROPE_THETA = 10000.0
A_GROUPS = ((128, 1), (512, 4), (2048, 16))
N_EXPERTS = 8
TOP_K = 2

LANES = 128
SUBLANES = 8
ATTN_BLOCK = 128
ATTN_HEAD_GROUP = 4
SB_BLOCK = 256
SB_UNROLL = 4
SB_DEAD = -104.0
ROW_STEP = 256
GATHER_TILE = 256
DOWN_CHUNK = 512
VMEM_LIMIT = 56 * 1024 * 1024

_NT = (((1,), (1,)), ((), ()))


def _params(*semantics):
    return pltpu.CompilerParams(dimension_semantics=semantics, vmem_limit_bytes=VMEM_LIMIT)


def _sigmoid(x):
    return 1.0 / (1.0 + jnp.exp(-x))


def _residue_perm(n_rows, dil, inverse):
    n = n_rows // dil
    assert dil & (dil - 1) == 0 and n & (n - 1) == 0
    out_row = lax.broadcasted_iota(jnp.int32, (n_rows, n_rows), 0)
    in_row = lax.broadcasted_iota(jnp.int32, (n_rows, n_rows), 1)
    if inverse:
        src = (out_row & (dil - 1)) * n + (out_row >> (dil.bit_length() - 1))
    else:
        src = (out_row & (n - 1)) * dil + (out_row >> (n.bit_length() - 1))
    return in_row == src


def _norm_kernel(h_ref, g_ref, o_ref):
    h = h_ref[...]
    y = h * lax.rsqrt(jnp.mean(h * h, axis=-1, keepdims=True) + EPS)
    o_ref[...] = (y * g_ref[...]).astype(o_ref.dtype)


def _rmsnorm(h, gain, tm=256):
    T, D = h.shape
    row = pl.BlockSpec((tm, D), lambda i: (i, 0))
    return pl.pallas_call(
        _norm_kernel,
        grid=(T // tm,),
        in_specs=[row, pl.BlockSpec((1, D), lambda i: (0, 0))],
        out_specs=row,
        out_shape=jax.ShapeDtypeStruct((T, D), BF16),
        compiler_params=_params("parallel"),
        name="rmsnorm",
    )(h, gain.reshape(1, D))


def _norm_dilated_kernel(h_ref, g_ref, *o_refs, dils):
    h = h_ref[...]
    tm = h.shape[0]
    y = (h * lax.rsqrt(jnp.mean(h * h, axis=-1, keepdims=True) + EPS) * g_ref[...]).astype(BF16)
    for o_ref, dil in zip(o_refs, dils):
        if dil == 1:
            o_ref[0, 0] = y
        else:
            perm = _residue_perm(tm, dil, inverse=False).astype(BF16)
            yp = jnp.dot(perm, y, preferred_element_type=F32).astype(BF16)
            n = tm // dil
            for r in range(dil):
                o_ref[0, r] = yp[r * n:(r + 1) * n]


def _rmsnorm_dilated(h, gain, B, S, dils, tm=256):
    T, D = h.shape
    bpb = S // tm
    return pl.pallas_call(
        functools.partial(_norm_dilated_kernel, dils=tuple(dils)),
        grid=(T // tm,),
        in_specs=[pl.BlockSpec((tm, D), lambda i: (i, 0)), pl.BlockSpec((1, D), lambda i: (0, 0))],
        out_specs=[pl.BlockSpec((1, d, tm // d, D), lambda i: (i // bpb, 0, i % bpb, 0)) for d in dils],
        out_shape=[jax.ShapeDtypeStruct((B, d, S // d, D), BF16) for d in dils],
        compiler_params=_params("parallel"),
        name="rmsnorm_dilated",
    )(h, gain.reshape(1, D))


def _mm_kernel(x_ref, w_ref, *rest, has_res):
    acc = jnp.dot(x_ref[...], w_ref[...].astype(BF16), preferred_element_type=F32)
    if has_res:
        r_ref, o_ref = rest
        acc = r_ref[...] + acc
    else:
        (o_ref,) = rest
    o_ref[...] = acc.astype(o_ref.dtype)


def _matmul(x, w, out_dtype, residual=None, tm=1024, tn=512):
    M, K = x.shape
    N = w.shape[1]
    has_res = residual is not None
    in_specs = [pl.BlockSpec((tm, K), lambda i, j: (i, 0)),
                pl.BlockSpec((K, tn), lambda i, j: (0, j))]
    ins = [x, w]
    if has_res:
        in_specs.append(pl.BlockSpec((tm, tn), lambda i, j: (i, j)))
        ins.append(residual)
    return pl.pallas_call(
        functools.partial(_mm_kernel, has_res=has_res),
        grid=(M // tm, N // tn),
        in_specs=in_specs,
        out_specs=pl.BlockSpec((tm, tn), lambda i, j: (i, j)),
        out_shape=jax.ShapeDtypeStruct((M, N), out_dtype),
        compiler_params=_params("parallel", "arbitrary"),
        name="matmul",
    )(*ins)


def _mm_heads_kernel(x_ref, w_ref, o_ref):
    acc = jnp.dot(x_ref[...], w_ref[...].astype(BF16), preferred_element_type=F32)
    for hh in range(o_ref.shape[0]):
        o_ref[hh] = acc[:, hh * HEAD_DIM:(hh + 1) * HEAD_DIM].astype(o_ref.dtype)


def _matmul_heads(x, w, B, S, tm=1024, tn=512):
    M, K = x.shape
    N = w.shape[1]
    bpb = S // tm
    hpt = tn // HEAD_DIM
    return pl.pallas_call(
        _mm_heads_kernel,
        grid=(M // tm, N // tn),
        in_specs=[pl.BlockSpec((tm, K), lambda i, j: (i, 0)),
                  pl.BlockSpec((K, tn), lambda i, j: (0, j))],
        out_specs=pl.BlockSpec((None, hpt, tm, HEAD_DIM), lambda i, j: (i // bpb, j, i % bpb, 0)),
        out_shape=jax.ShapeDtypeStruct((B, N // HEAD_DIM, S, HEAD_DIM), BF16),
        compiler_params=_params("parallel", "arbitrary"),
        name="matmul_heads",
    )(x, w)


def _qkv_kernel(x_ref, w_ref, cos_ref, sin_ref, gq_ref, gk_ref, o_ref, acc_a, acc_b, *,
                q_tiles, qk_tiles, n_tiles):
    j = pl.program_id(1)

    def matmul(acc_ref):
        acc_ref[...] = jnp.dot(x_ref[...], w_ref[...].astype(BF16), preferred_element_type=F32)

    def finish(acc_ref):
        jp = j - 1
        is_qk = jp < qk_tiles
        gain = jnp.where(jp < q_tiles, gq_ref[...], gk_ref[...])
        cos = cos_ref[...]
        sin = sin_ref[...]
        for hh in range(acc_ref.shape[1] // HEAD_DIM):
            sl = slice(hh * HEAD_DIM, (hh + 1) * HEAD_DIM)
            blk = acc_ref[:, sl]
            y = blk * lax.rsqrt(jnp.mean(blk * blk, axis=-1, keepdims=True) + EPS) * gain
            y = y * cos + pltpu.roll(y, HEAD_DIM // 2, 1) * sin
            o_ref[:, sl] = jnp.where(is_qk, y, blk).astype(o_ref.dtype)

    even = j % 2 == 0

    @pl.when(j == 0)
    def _():
        matmul(acc_a)

    @pl.when((j > 0) & (j < n_tiles) & even)
    def _():
        matmul(acc_a)
        finish(acc_b)

    @pl.when((j < n_tiles) & jnp.logical_not(even))
    def _():
        matmul(acc_b)
        finish(acc_a)

    @pl.when(j == n_tiles)
    def _():
        finish(acc_b if n_tiles % 2 == 0 else acc_a)


def _rope_tables(S, dil):
    half = HEAD_DIM // 2
    inv = ROPE_THETA ** (-jnp.arange(half, dtype=F32) / half)
    pos = jnp.arange(S, dtype=F32).reshape(S // dil, dil).T.reshape(S)
    ang = pos[:, None] * inv[None, :]
    cos, sin = jnp.cos(ang), jnp.sin(ang)
    return jnp.concatenate([cos, cos], axis=-1), jnp.concatenate([-sin, sin], axis=-1)


def _qkv_proj(x, w, g_q, g_k, S, dil, group, n_groups, tm=1024, tn=512):
    M, K = x.shape
    hw = w.shape[1] // (3 * n_groups)
    tpg = hw // tn
    cos, sin = _rope_tables(S, dil)
    pos_blocks = S // tm
    n_tiles = 3 * tpg

    def w_col(i, j):
        jc = jnp.minimum(j, n_tiles - 1)
        return (0, ((jc // tpg) * n_groups + group) * tpg + jc % tpg)

    return pl.pallas_call(
        functools.partial(_qkv_kernel, q_tiles=tpg, qk_tiles=2 * tpg, n_tiles=n_tiles),
        grid=(M // tm, n_tiles + 1),
        in_specs=[pl.BlockSpec((tm, K), lambda i, j: (i, 0)),
                  pl.BlockSpec((K, tn), w_col),
                  pl.BlockSpec((tm, HEAD_DIM), lambda i, j: (i % pos_blocks, 0)),
                  pl.BlockSpec((tm, HEAD_DIM), lambda i, j: (i % pos_blocks, 0)),
                  pl.BlockSpec((1, HEAD_DIM), lambda i, j: (0, 0)),
                  pl.BlockSpec((1, HEAD_DIM), lambda i, j: (0, 0))],
        out_specs=pl.BlockSpec((tm, tn), lambda i, j: (i, jnp.maximum(j - 1, 0))),
        out_shape=jax.ShapeDtypeStruct((M, 3 * hw), BF16),
        scratch_shapes=[pltpu.VMEM((tm, tn), F32), pltpu.VMEM((tm, tn), F32)],
        compiler_params=_params("parallel", "arbitrary"),
        name=f"qkv_proj_d{dil}",
    )(x, w, cos, sin, g_q.reshape(1, HEAD_DIM), g_k.reshape(1, HEAD_DIM))


def _dil_attn_kernel(q_ref, kp_ref, kc_ref, vp_ref, vc_ref, o_ref, st_ref, *, n_heads, scale):
    n = pl.program_id(2)
    blk = q_ref.shape[0]
    i = lax.broadcasted_iota(jnp.int32, (blk, blk), 0)
    j = lax.broadcasted_iota(jnp.int32, (blk, blk), 1)
    mask_p = (j >= i) & (n > 0)
    mask_c = j <= i
    lane = lax.broadcasted_iota(jnp.int32, (blk, LANES), 1)
    stats = jnp.zeros((blk, LANES), F32)
    for h0 in range(0, n_heads, ATTN_HEAD_GROUP):
        sls = [slice(h * HEAD_DIM, (h + 1) * HEAD_DIM) for h in range(h0, h0 + ATTN_HEAD_GROUP)]
        s_ps = [lax.dot_general(q_ref[:, sl], kp_ref[:, sl], _NT, preferred_element_type=F32) for sl in sls]
        s_cs = [lax.dot_general(q_ref[:, sl], kc_ref[:, sl], _NT, preferred_element_type=F32) for sl in sls]
        p_ps, p_cs, ls = [], [], []
        for u, (s_p, s_c) in enumerate(zip(s_ps, s_cs)):
            s_p = jnp.where(mask_p, s_p * scale, -jnp.inf)
            s_c = jnp.where(mask_c, s_c * scale, -jnp.inf)
            m = jnp.maximum(jnp.max(s_p, axis=-1, keepdims=True), jnp.max(s_c, axis=-1, keepdims=True))
            p_p = jnp.exp(s_p - m)
            p_c = jnp.exp(s_c - m)
            l = jnp.sum(p_p, axis=-1, keepdims=True) + jnp.sum(p_c, axis=-1, keepdims=True)
            p_ps.append(p_p.astype(BF16))
            p_cs.append(p_c.astype(BF16))
            ls.append(l)
            stats = jnp.where(lane == h0 + u, m + jnp.log(l), stats)
        for sl, p_p, p_c, l in zip(sls, p_ps, p_cs, ls):
            o = (jnp.dot(p_p, vp_ref[:, sl], preferred_element_type=F32)
                 + jnp.dot(p_c, vc_ref[:, sl], preferred_element_type=F32))
            o_ref[:, sl] = (o / l).astype(o_ref.dtype)
    st_ref[...] = stats


def _dilated_attention_group(qkv_g, B, S, dil, n_heads):
    hw = n_heads * HEAD_DIM
    L = S // dil
    nb = L // ATTN_BLOCK
    a = qkv_g.reshape(B, dil, L, 3 * hw)
    blk = (None, None, ATTN_BLOCK, hw)
    cur = lambda c: (lambda b, r, n: (b, r, n, c))
    prev = lambda c: (lambda b, r, n: (b, r, jnp.maximum(n - 1, 0), c))
    return pl.pallas_call(
        functools.partial(_dil_attn_kernel, n_heads=n_heads, scale=1.0 / math.sqrt(HEAD_DIM)),
        grid=(B, dil, nb),
        in_specs=[pl.BlockSpec(blk, cur(0)),
                  pl.BlockSpec(blk, prev(1)), pl.BlockSpec(blk, cur(1)),
                  pl.BlockSpec(blk, prev(2)), pl.BlockSpec(blk, cur(2))],
        out_specs=[pl.BlockSpec(blk, lambda b, r, n: (b, r, n, 0)),
                   pl.BlockSpec((None, None, ATTN_BLOCK, LANES), lambda b, r, n: (b, r, n, 0))],
        out_shape=[jax.ShapeDtypeStruct((B, dil, L, hw), BF16),
                   jax.ShapeDtypeStruct((B, dil, L, LANES), F32)],
        compiler_params=_params("parallel", "parallel", "arbitrary"),
        name=f"dilated_attn_d{dil}",
    )(a, a, a, a, a)


def _merge_kernel(*refs, dils, n_heads):
    G = len(dils)
    o_refs, s_refs, out_ref = refs[:G], refs[G:2 * G], refs[2 * G]
    tm, hw = out_ref.shape
    o, s = [], []
    for o_ref, s_ref, dil in zip(o_refs, s_refs, dils):
        o_g = o_ref[...].reshape(tm, hw)
        s_g = s_ref[...].reshape(tm, LANES)
        if dil == 1:
            o_g = o_g.astype(F32)
        else:
            pinv = _residue_perm(tm, dil, inverse=True)
            o_g = jnp.dot(pinv.astype(BF16), o_g, preferred_element_type=F32)
            s_g = jnp.dot(pinv.astype(F32), s_g, preferred_element_type=F32,
                          precision=lax.Precision.HIGHEST)
        o.append(o_g)
        s.append(s_g)
    m = functools.reduce(jnp.maximum, s)
    e = [jnp.exp(x - m) for x in s]
    den = functools.reduce(lambda a, b: a + b, e)
    w = [x / den for x in e]
    for h in range(n_heads):
        sl = slice(h * HEAD_DIM, (h + 1) * HEAD_DIM)
        acc = w[0][:, h:h + 1] * o[0][:, sl]
        for g in range(1, G):
            acc = acc + w[g][:, h:h + 1] * o[g][:, sl]
        out_ref[:, sl] = acc.astype(out_ref.dtype)


def _merge_groups(os, sts, dils, B, S, n_heads, tm=256):
    hw = n_heads * HEAD_DIM
    bpb = S // tm
    spec = lambda d, w: pl.BlockSpec((None, d, tm // d, w), lambda i: (i // bpb, 0, i % bpb, 0))
    return pl.pallas_call(
        functools.partial(_merge_kernel, dils=tuple(dils), n_heads=n_heads),
        grid=(B * bpb,),
        in_specs=[spec(d, hw) for d in dils] + [spec(d, LANES) for d in dils],
        out_specs=pl.BlockSpec((tm, hw), lambda i: (i, 0)),
        out_shape=jax.ShapeDtypeStruct((B * S, hw), BF16),
        compiler_params=_params("parallel"),
        name="merge_groups",
    )(*os, *sts)


def _sb_kernel(q_ref, k_ref, v_ref, o_ref, *, scale):
    tq = SB_BLOCK
    row = lax.broadcasted_iota(jnp.int32, (tq, tq), 0)
    col = lax.broadcasted_iota(jnp.int32, (tq, tq), 1)
    strict = col < row
    tri = (row > col).astype(BF16)

    def tiles(qs, jobs, carries):
        starts = [pl.multiple_of(kb * tq, tq) for _, kb, _ in jobs]
        zs = [lax.dot_general(qs[j], k_ref[pl.ds(s, tq), :], _NT, preferred_element_type=F32) * scale
              for (j, _, _), s in zip(jobs, starts)]
        lks, lss = [], []
        for (_, _, diag), z in zip(jobs, zs):
            lk = -(jnp.maximum(z, 0.0) + jnp.log(1.0 + jnp.exp(-jnp.abs(z))))
            lss.append(z + lk)
            lks.append(jnp.where(strict, lk, 0.0) if diag else lk)
        sufs = [jnp.dot(lk.astype(BF16), tri, preferred_element_type=F32) for lk in lks]
        accs = [c[0] for c in carries]
        runs = [c[1] for c in carries]
        ps = []
        for u, (j, _, diag) in enumerate(jobs):
            a = jnp.exp(lss[u] + (sufs[u] + runs[j]))
            ps.append((jnp.where(strict, a, 0.0) if diag else a).astype(BF16))
            runs[j] = runs[j] + jnp.sum(lks[u], axis=1, keepdims=True)
        for u, (j, _, _) in enumerate(jobs):
            accs[j] = accs[j] + jnp.dot(ps[u], v_ref[pl.ds(starts[u], tq), :], preferred_element_type=F32)
        return tuple(zip(accs, runs))

    alive = lambda r: (jnp.max(r) > SB_DEAD).astype(jnp.int32)

    def finish(q, qi, carry):
        rest = jnp.maximum(qi - 1, 0)
        rem = rest % SB_UNROLL

        def walk(n, first, count, state):
            def body(c):
                t, acc, run, _ = c
                jobs = [(0, first - t * n - u, False) for u in range(n)]
                ((acc, run),) = tiles([q], jobs, [(acc, run)])
                return t + 1, acc, run, alive(run)
            return lax.while_loop(lambda c: (c[0] < count) & (c[3] != 0), body, (0,) + state)[1:]

        state = carry + (alive(carry[1]),)
        state = walk(1, rest - 1, rem, state)
        state = walk(SB_UNROLL, rest - 1 - rem, rest // SB_UNROLL, state)
        return state[0]

    def q_pair(p, _):
        qa, qb = 2 * p, 2 * p + 1
        rows = [pl.ds(pl.multiple_of(qi * tq, tq), tq) for qi in (qa, qb)]
        qs = [q_ref[r, :] for r in rows]
        zero = (jnp.zeros((tq, HEAD_DIM), F32), jnp.zeros((tq, 1), F32))
        first_pair = [(0, qa, True), (1, qb, True), (1, qa, False)]
        later_pair = [(0, qa, True), (0, qa - 1, False), (1, qb, True), (1, qa, False)]
        carries = lax.cond(p == 0, functools.partial(tiles, qs, first_pair),
                           functools.partial(tiles, qs, later_pair), (zero, zero))
        for q, qi, r, carry in zip(qs, (qa, qb), rows, carries):
            o_ref[r, :] = finish(q, qi, carry).astype(o_ref.dtype)
        return 0

    lax.fori_loop(0, q_ref.shape[0] // (2 * tq), q_pair, 0)


def _stick_breaking(q, kv, n_heads):
    B, _, S, _ = q.shape
    head = lambda off: pl.BlockSpec((None, None, S, HEAD_DIM), lambda b, h: (b, off + h, 0, 0))
    out = pl.pallas_call(
        functools.partial(_sb_kernel, scale=1.0 / math.sqrt(HEAD_DIM)),
        grid=(B, n_heads),
        in_specs=[head(0), head(0), head(n_heads)],
        out_specs=pl.BlockSpec((None, S, HEAD_DIM), lambda b, h: (b, 0, h)),
        out_shape=jax.ShapeDtypeStruct((B, S, n_heads * HEAD_DIM), BF16),
        compiler_params=_params("parallel", "arbitrary"),
        name="stick_breaking",
    )(q, kv, kv)
    return out.reshape(B * S, n_heads * HEAD_DIM)


def _swiglu_kernel(te_ref, tv_ref, x_ref, wg_ref, wu_ref, wd_ref, sc_ref, o_ref, *, use_scale):
    t = pl.program_id(0)
    f = pl.program_id(1)
    tm = x_ref.shape[0]
    steps = (tv_ref[t] + ROW_STEP - 1) // ROW_STEP

    def body(m):
        x = x_ref[:m, :]
        g = jnp.dot(x, wg_ref[...].astype(BF16), preferred_element_type=F32)
        u = jnp.dot(x, wu_ref[...].astype(BF16), preferred_element_type=F32)
        mid = (g * _sigmoid(g) * u).astype(BF16)

        @pl.when(f == 0)
        def _():
            o_ref[...] = jnp.zeros_like(o_ref)

        for c in range(0, o_ref.shape[1], DOWN_CHUNK):
            sl = slice(c, c + DOWN_CHUNK)
            o_ref[:m, sl] += jnp.dot(mid, wd_ref[:, sl].astype(BF16), preferred_element_type=F32)

        if use_scale:
            @pl.when(f == pl.num_programs(1) - 1)
            def _():
                o_ref[:m, :] *= sc_ref[:m, :]

    for k in range(1, tm // ROW_STEP + 1):
        pl.when(steps == k)(functools.partial(body, k * ROW_STEP))

    @pl.when((steps == 0) & (f == 0))
    def _():
        o_ref[...] = jnp.zeros_like(o_ref)


def _swiglu(x, w_gate, w_up, w_down, tile_expert, tile_rows, row_scale, tm=1024, tf=256):
    P, D = x.shape
    E, _, F = w_gate.shape
    nf = F // tf
    use_scale = row_scale is not None
    if not use_scale:
        row_scale = jnp.ones((P, 1), F32)
    fidx = lambda t, f, tv: jnp.where(tv[t] != 0, f, nf - 1)
    grid_spec = pltpu.PrefetchScalarGridSpec(
        num_scalar_prefetch=2,
        grid=(P // tm, nf),
        in_specs=[pl.BlockSpec((tm, D), lambda t, f, te, tv: (t, 0), pipeline_mode=pl.Buffered(1)),
                  pl.BlockSpec((None, D, tf), lambda t, f, te, tv: (te[t], 0, fidx(t, f, tv))),
                  pl.BlockSpec((None, D, tf), lambda t, f, te, tv: (te[t], 0, fidx(t, f, tv))),
                  pl.BlockSpec((None, tf, D), lambda t, f, te, tv: (te[t], fidx(t, f, tv), 0)),
                  pl.BlockSpec((tm, 1), lambda t, f, te, tv: (t, 0))],
        out_specs=pl.BlockSpec((tm, D), lambda t, f, te, tv: (t, 0), pipeline_mode=pl.Buffered(1)),
    )
    return pl.pallas_call(
        functools.partial(_swiglu_kernel, use_scale=use_scale),
        grid_spec=grid_spec,
        out_shape=jax.ShapeDtypeStruct((P, D), F32),
        compiler_params=_params("parallel", "arbitrary"),
        name="swiglu",
    )(tile_expert, tile_rows, x, w_gate, w_up, w_down, row_scale)


def _router_kernel(h_ref, g_ref, w_ref, xc_ref, idx_ref, p_ref, *, n_experts):
    h = h_ref[...]
    tm, D = h.shape
    x = h * lax.rsqrt(jnp.mean(h * h, axis=-1, keepdims=True) + EPS) * g_ref[...]
    C = D // LANES
    pitch = _chunk_pitch(D)
    for c in range(C):
        xc_ref[pl.ds(c, tm, stride=pitch), :] = x[:, c * LANES:(c + 1) * LANES]
    for c in range(C, pitch):
        xc_ref[pl.ds(c, tm, stride=pitch), :] = jnp.zeros((tm, LANES), F32)
    logits = jnp.dot(x, w_ref[...], preferred_element_type=F32, precision=lax.Precision.HIGHEST)
    lane = lax.broadcasted_iota(jnp.int32, logits.shape, 1)
    logits = jnp.where(lane < n_experts, logits, -jnp.inf)
    m1 = jnp.max(logits, axis=-1, keepdims=True)
    i1 = jnp.min(jnp.where(logits == m1, lane, LANES), axis=-1, keepdims=True)
    rest = jnp.where(lane == i1, -jnp.inf, logits)
    m2 = jnp.max(rest, axis=-1, keepdims=True)
    i2 = jnp.min(jnp.where(rest == m2, lane, LANES), axis=-1, keepdims=True)
    e = jnp.exp(m2 - m1)
    p1 = 1.0 / (1.0 + e)
    p2 = e / (1.0 + e)
    idx_ref[...] = jnp.where(lane == 0, i1, jnp.where(lane == 1, i2, 0))
    p_ref[...] = jnp.where(lane == 0, p1, jnp.where(lane == 1, p2, 0.0))


def _chunk_pitch(D):
    return D // LANES + SUBLANES


def _norm_router(h, gain, w_router, tm=256):
    T, D = h.shape
    E = w_router.shape[1]
    C = _chunk_pitch(D)
    w = jnp.pad(w_router, ((0, 0), (0, LANES - E)))
    xc, idx, prob = pl.pallas_call(
        functools.partial(_router_kernel, n_experts=E),
        grid=(T // tm,),
        in_specs=[pl.BlockSpec((tm, D), lambda i: (i, 0)),
                  pl.BlockSpec((1, D), lambda i: (0, 0)),
                  pl.BlockSpec((D, LANES), lambda i: (0, 0))],
        out_specs=[pl.BlockSpec((tm * C, LANES), lambda i: (i, 0)),
                   pl.BlockSpec((tm, LANES), lambda i: (i, 0)),
                   pl.BlockSpec((tm, LANES), lambda i: (i, 0))],
        out_shape=[jax.ShapeDtypeStruct((T * C, LANES), F32),
                   jax.ShapeDtypeStruct((T, LANES), jnp.int32),
                   jax.ShapeDtypeStruct((T, LANES), F32)],
        compiler_params=_params("parallel"),
        name="norm_router",
    )(h, gain.reshape(1, D), w)
    return xc, idx[:, :TOP_K], prob[:, :TOP_K]


def _route_tables(top_i, top_p, tm, n_tiles):
    T = top_i.shape[0]
    A = T * TOP_K
    flat_e = top_i.reshape(A)
    flat_p = top_p.reshape(A)
    experts = jnp.arange(N_EXPERTS, dtype=jnp.int32)
    onehot = flat_e[:, None] == experts[None, :]
    running = jnp.cumsum(onehot.astype(jnp.int32), axis=0)
    counts = running[-1]
    pick = lambda table: jnp.sum(jnp.where(onehot, table, 0), axis=1)
    rank = pick(running) - 1
    tiles_per = (counts + tm - 1) // tm
    per = (counts + jnp.maximum(tiles_per, 1) - 1) // jnp.maximum(tiles_per, 1)
    tile_end = jnp.cumsum(tiles_per)
    tile_start = tile_end - tiles_per
    per_a = jnp.maximum(pick(per[None, :]), 1)
    pos = ((pick(tile_start[None, :]) + rank // per_a) * tm + rank % per_a).reshape(T, TOP_K)
    tile_ids = jnp.arange(n_tiles, dtype=jnp.int32)
    used = tile_end[-1]
    te = jnp.minimum(jnp.sum(tile_ids[:, None] >= tile_end[None, :], axis=1, dtype=jnp.int32), N_EXPERTS - 1)
    te_hot = te[:, None] == experts[None, :]
    tpick = lambda table: jnp.sum(jnp.where(te_hot, table[None, :], 0), axis=1)
    sorted_start = jnp.cumsum(counts) - counts
    before = (tile_ids - tpick(tile_start)) * tpick(per)
    tile_rows = jnp.where(tile_ids < used, jnp.clip(tpick(counts) - before, 0, tpick(per)), 0).astype(jnp.int32)
    last = jnp.sum((used - 1) >= tile_end, dtype=jnp.int32)
    tile_expert = jnp.where(tile_rows != 0, te, jnp.minimum(last, N_EXPERTS - 1))
    order = jnp.argsort(flat_e, stable=True).astype(jnp.int32)
    in_tile = jnp.arange(tm, dtype=jnp.int32)[None, :]
    valid = (in_tile < tile_rows[:, None]).reshape(n_tiles * tm)
    src = (tpick(sorted_start) + before)[:, None] + in_tile
    assign = order[jnp.clip(src, 0, A - 1).reshape(n_tiles * tm)]
    row_token = jnp.where(valid, assign // TOP_K, 0)
    row_scale = jnp.where(valid, flat_p[assign], 0.0)
    return row_token, row_scale.reshape(n_tiles * tm, 1), pos, tile_expert, tile_rows


def _gather_kernel(tok_ref, need_ref, x_hbm, o_ref, buf, sem):
    tm, D = o_ref.shape
    C = D // LANES
    pitch = _chunk_pitch(D)
    t = pl.program_id(0)

    def row_copy(tile, r):
        slot = tile % 2
        src = pl.multiple_of(tok_ref[tile * tm + r] * pitch, SUBLANES)
        dst = pl.multiple_of(r * pitch, SUBLANES)
        return pltpu.make_async_copy(x_hbm.at[pl.ds(src, C)], buf.at[slot, pl.ds(dst, C)], sem.at[slot])

    def start_tile(tile):
        def body(r8, c):
            for j in range(SUBLANES):
                row_copy(tile, r8 * SUBLANES + j).start(priority=j % 2)
            return c
        lax.fori_loop(0, tm // SUBLANES, body, 0)

    @pl.when((t == 0) & (need_ref[0] != 0))
    def _():
        start_tile(0)

    nxt = jnp.minimum(t + 1, pl.num_programs(0) - 1)

    @pl.when((t + 1 < pl.num_programs(0)) & (need_ref[nxt] != 0))
    def _():
        start_tile(t + 1)

    @pl.when(need_ref[t] != 0)
    def _():
        def wait(r, c):
            row_copy(t, r).wait()
            return c

        lax.fori_loop(0, tm, wait, 0, unroll=8)
        slot = t % 2
        for c in range(C):
            o_ref[:, c * LANES:(c + 1) * LANES] = buf[slot, pl.ds(c, tm, stride=pitch), :].astype(o_ref.dtype)

    @pl.when(need_ref[t] == 0)
    def _():
        o_ref[...] = jnp.zeros_like(o_ref)


def _gather_rows(xc, row_token, need, D, tm):
    P = row_token.shape[0]
    grid_spec = pltpu.PrefetchScalarGridSpec(
        num_scalar_prefetch=2,
        grid=(P // tm,),
        in_specs=[pl.BlockSpec(memory_space=pl.ANY)],
        out_specs=pl.BlockSpec((tm, D), lambda t, tok, need: (t, 0)),
        scratch_shapes=[pltpu.VMEM((2, tm * _chunk_pitch(D), LANES), F32), pltpu.SemaphoreType.DMA((2,))],
    )
    return pl.pallas_call(
        _gather_kernel,
        grid_spec=grid_spec,
        out_shape=jax.ShapeDtypeStruct((P, D), BF16),
        compiler_params=_params("arbitrary"),
        name="gather_rows",
    )(row_token, need, xc)


def _combine_kernel(p0_ref, p1_ref, y_hbm, o_ref, buf, sem):
    tm = o_ref.shape[0]
    t = pl.program_id(0)

    def copies(tile, r):
        slot = tile % 2
        return tuple(
            pltpu.make_async_copy(y_hbm.at[pl.ds(p_ref[tile * tm + r], 1)], buf.at[slot, j, pl.ds(r, 1)],
                                  sem.at[slot])
            for j, p_ref in enumerate((p0_ref, p1_ref)))

    def start_tile(tile):
        def body(r, c):
            for j, cp in enumerate(copies(tile, r)):
                cp.start(priority=j)
            return c
        lax.fori_loop(0, tm, body, 0)

    @pl.when(t == 0)
    def _():
        start_tile(0)

    @pl.when(t + 1 < pl.num_programs(0))
    def _():
        start_tile(t + 1)

    def wait(r, c):
        for cp in copies(t, r):
            cp.wait()
        return c

    lax.fori_loop(0, tm, wait, 0)
    slot = t % 2
    o_ref[...] = buf[slot, 0] + buf[slot, 1]


def _combine_rows(y, pos, tm=256):
    T = pos.shape[0]
    D = y.shape[1]
    grid_spec = pltpu.PrefetchScalarGridSpec(
        num_scalar_prefetch=2,
        grid=(T // tm,),
        in_specs=[pl.BlockSpec(memory_space=pl.ANY)],
        out_specs=pl.BlockSpec((tm, D), lambda t, p0, p1: (t, 0)),
        scratch_shapes=[pltpu.VMEM((2, 2, tm, D), F32), pltpu.SemaphoreType.DMA((2,))],
    )
    return pl.pallas_call(
        _combine_kernel,
        grid_spec=grid_spec,
        out_shape=jax.ShapeDtypeStruct((T, D), F32),
        compiler_params=_params("arbitrary"),
        name="combine_rows",
    )(pos[:, 0], pos[:, 1], y)


def _ple_kernel(h_ref, d_ref, p_ref, g_ref, wgd_ref, wgu_ref, wup_ref, *rest, n_norm):
    gn_refs = rest[:n_norm]
    o_ref = rest[n_norm]
    on_refs = rest[n_norm + 1:]
    h = h_ref[...] + d_ref[...]
    y = h * lax.rsqrt(jnp.mean(h * h, axis=-1, keepdims=True) + EPS) * g_ref[...]
    t = jnp.dot(y.astype(BF16), wgd_ref[...], preferred_element_type=F32)
    gate = _sigmoid(jnp.dot(t.astype(BF16), wgu_ref[...], preferred_element_type=F32))
    up = jnp.dot(p_ref[...].astype(BF16), wup_ref[...], preferred_element_type=F32)
    h = h + gate * up
    o_ref[...] = h
    if n_norm:
        y = h * lax.rsqrt(jnp.mean(h * h, axis=-1, keepdims=True) + EPS)
        for gn_ref, on_ref in zip(gn_refs, on_refs):
            on_ref[...] = (y * gn_ref[...]).astype(on_ref.dtype)


def _ple(h, delta, p_i, g_norm, w_up, w_gdown, w_gup, next_gains, tm=256):
    T, D = h.shape
    R = p_i.shape[1]
    row = pl.BlockSpec((tm, D), lambda i: (i, 0))
    gain = pl.BlockSpec((1, D), lambda i: (0, 0))
    n_norm = len(next_gains)
    outs = pl.pallas_call(
        functools.partial(_ple_kernel, n_norm=n_norm),
        grid=(T // tm,),
        in_specs=[row, row, pl.BlockSpec((tm, R), lambda i: (i, 0)), gain,
                  pl.BlockSpec((D, R), lambda i: (0, 0)),
                  pl.BlockSpec((R, D), lambda i: (0, 0)),
                  pl.BlockSpec((R, D), lambda i: (0, 0))] + [gain] * n_norm,
        out_specs=[row] * (1 + n_norm),
        out_shape=[jax.ShapeDtypeStruct((T, D), F32)] + [jax.ShapeDtypeStruct((T, D), BF16)] * n_norm,
        compiler_params=_params("parallel"),
        name="per_layer_embedding",
    )(h, delta, p_i, g_norm.reshape(1, D), w_gdown.astype(BF16), w_gup.astype(BF16),
      w_up.astype(BF16), *[g.reshape(1, D) for g in next_gains])
    return list(outs)


def kernel(x, p, norm_mix, norm_ffn, norm_ple, a_w_qkv, a_q_norm, a_k_norm, a_w_o, kv_norm, w_kv, b_w_q, b_w_o, ffn_w_gate, ffn_w_up, ffn_w_down, moe_w_router, moe_w_gate, moe_w_up, moe_w_down, ple_w_up, ple_w_gdown, ple_w_gup):
    B, S, D = x.shape
    T = B * S
    assert p.shape[0] == 2 and a_w_qkv.shape[0] == 1 and b_w_q.shape[0] == 1
    assert moe_w_router.shape[-1] == N_EXPERTS and moe_w_gate.shape[1] == N_EXPERTS
    G = len(A_GROUPS)
    a_heads = a_w_o.shape[1] // HEAD_DIM
    b_heads = b_w_q.shape[2] // HEAD_DIM
    x2 = x.reshape(T, D)
    p2 = p.reshape(2, T, p.shape[-1])

    dils = [dil for _, dil in A_GROUPS]
    hns = _rmsnorm_dilated(x2, norm_mix[0], B, S, dils)
    os, sts = [], []
    for g, (span, dil) in enumerate(A_GROUPS):
        assert span // dil == ATTN_BLOCK
        qkv_g = _qkv_proj(hns[g].reshape(T, D), a_w_qkv[0], a_q_norm[0], a_k_norm[0], S, dil, g, G)
        o_g, st_g = _dilated_attention_group(qkv_g, B, S, dil, a_heads)
        os.append(o_g)
        sts.append(st_g)
    attn = _merge_groups(os, sts, dils, B, S, a_heads)
    h = _matmul(attn, a_w_o[0], F32, residual=x2, tn=1024)

    hn = _rmsnorm(h, norm_ffn[0])
    tm = 1024
    delta = _swiglu(hn, ffn_w_gate, ffn_w_up, ffn_w_down, jnp.zeros((T // tm,), jnp.int32),
                    jnp.full((T // tm,), tm, jnp.int32), None, tm=tm)
    h, hn_kv, hn_q = _ple(h, delta, p2[0], norm_ple[0], ple_w_up[0], ple_w_gdown[0], ple_w_gup[0],
                          [kv_norm, norm_mix[1]])

    kv = _matmul_heads(hn_kv, w_kv, B, S)
    q = _matmul_heads(hn_q, b_w_q[0], B, S)
    sb = _stick_breaking(q, kv, b_heads)
    h = _matmul(sb, b_w_o[0], F32, residual=h)

    xc, top_i, top_p = _norm_router(h, norm_ffn[1], moe_w_router[0])
    n_tiles = (T * TOP_K) // tm + N_EXPERTS
    row_token, row_scale, pos, tile_expert, tile_rows = _route_tables(top_i, top_p, tm, n_tiles)
    assert GATHER_TILE % ROW_STEP == 0
    first_row = jnp.arange(tm // GATHER_TILE, dtype=jnp.int32)[None, :] * GATHER_TILE
    need = (first_row < tile_rows[:, None]).astype(jnp.int32).reshape(-1)
    xs = _gather_rows(xc, row_token, need, D, GATHER_TILE)
    ys = _swiglu(xs, moe_w_gate[0], moe_w_up[0], moe_w_down[0], tile_expert, tile_rows, row_scale, tm=tm)
    delta = _combine_rows(ys, pos)
    (h,) = _ple(h, delta, p2[1], norm_ple[1], ple_w_up[1], ple_w_gdown[1], ple_w_gup[1], [])
    return h.reshape(B, S, D)
```
